```python
import math
import jax, jax.numpy as jnp
from jax import lax
import numpy as np

D_MODEL = 1024
BATCH = 16
SEQ = 4096
DEPTH = 2
DEC_BATCH = 16
DEC_SEQ = 64
PAST_LEN = 2048

CHUNK = 64
CONV_W = 4
EPS = 1e-6
SSD_HEADS = 6
SSD_HEAD_DIM = 64
SSD_D = SSD_HEADS * SSD_HEAD_DIM
SSD_GROUPS = 2
SSD_STATE = 64
SSD_CONV_DIM = SSD_D + 2 * SSD_GROUPS * SSD_STATE
GMLP_GROUPS = 4
GMLP_GROUP_DIM = 64
GMLP_D = GMLP_GROUPS * GMLP_GROUP_DIM
GMLP_CHUNK = 128
GDN_HEADS = 6
GDN_KEY_DIM = 64
GDN_VAL_DIM = 64
GDN_QK_D = GDN_HEADS * GDN_KEY_DIM
GDN_V_D = GDN_HEADS * GDN_VAL_DIM
GDN_CONV_DIM = 2 * GDN_QK_D + GDN_V_D
MIX_D = SSD_D + GMLP_D + GDN_V_D
IN_SIZES = (SSD_CONV_DIM, SSD_D, SSD_HEADS, GMLP_D, GMLP_D, GDN_CONV_DIM, GDN_HEADS, GDN_HEADS, GDN_V_D)
IN_D = SSD_CONV_DIM + SSD_D + SSD_HEADS + 2 * GMLP_D + GDN_CONV_DIM + 2 * GDN_HEADS + GDN_V_D
N_EXPERTS = 16
N_EXPERT_GROUPS = 4
EXPERTS_PER_GROUP = N_EXPERTS // N_EXPERT_GROUPS
TOP_K = 2
D_EXPERT = 512
MOE_BLOCK = 128

kernel_name = 'hybrid_ssd_gmlp_gdn_moe_stream_step'

F32 = jnp.float32


def _rms(x, w):
    xf = x.astype(F32)
    y = xf * lax.rsqrt(jnp.mean(xf * xf, axis=-1, keepdims=True) + EPS)
    return (y * w.astype(F32)).astype(x.dtype)


def _l2(x):
    xf = x.astype(F32)
    return xf * lax.rsqrt(jnp.sum(xf * xf, axis=-1, keepdims=True) + EPS)


def _causal_conv(x, buf, w, b=None):
    xp = jnp.concatenate([buf.astype(x.dtype), x], axis=1)
    out = lax.conv_general_dilated(xp, w[:, None, :].astype(x.dtype), window_strides=(1,), padding='VALID',
                                   dimension_numbers=('NWC', 'WIO', 'NWC'), feature_group_count=x.shape[-1])
    if b is not None:
        out = out + b
    return out, xp[:, -(CONV_W - 1):]


def _ssd(x, dt, a_log, bm, cm, h0):
    bsz, L, H, P = x.shape
    Q = min(CHUNK, L)
    nc = L // Q
    rep = H // SSD_GROUPS
    bm = jnp.repeat(bm.astype(F32), rep, axis=2).reshape(bsz, nc, Q, H, SSD_STATE)
    cm = jnp.repeat(cm.astype(F32), rep, axis=2).reshape(bsz, nc, Q, H, SSD_STATE)
    dt = dt.astype(F32)
    a = (dt * -jnp.exp(a_log.astype(F32))).reshape(bsz, nc, Q, H)
    xdt = (x.astype(F32) * dt[..., None]).reshape(bsz, nc, Q, H, P)
    cum = jnp.cumsum(a, axis=2)
    causal = jnp.tril(jnp.ones((Q, Q), bool))[None, None, :, :, None]
    seg = cum[:, :, :, None, :] - cum[:, :, None, :, :]
    scores = jnp.einsum('bcihn,bcjhn->bcijh', cm, bm) * jnp.exp(jnp.where(causal, seg, -jnp.inf))
    y = jnp.einsum('bcijh,bcjhp->bcihp', scores, xdt)
    w_end = jnp.exp(cum[:, :, -1:, :] - cum)
    states = jnp.einsum('bcjhn,bcjhp->bchpn', bm * w_end[..., None], xdt)
    chunk_decay = jnp.exp(cum[:, :, -1, :])

    def step(h, inp):
        s, d = inp
        return h * d[:, :, None, None] + s, h

    h_end, h_start = lax.scan(step, h0.astype(F32), (jnp.moveaxis(states, 1, 0), jnp.moveaxis(chunk_decay, 1, 0)))
    h_start = jnp.moveaxis(h_start, 0, 1)
    y = y + jnp.einsum('bcihn,bchpn->bcihp', cm * jnp.exp(cum)[..., None], h_start)
    return y.reshape(bsz, L, H, P), h_end


def _gdn(q, k, v, g, beta, s0):
    bsz, L, H, K = q.shape
    V = v.shape[-1]
    Q = min(CHUNK, L)
    nc = L // Q

    def blk(t):
        t = t.astype(F32)
        return jnp.moveaxis(t.reshape((bsz, nc, Q) + t.shape[2:]), 3, 2)

    q = blk(q) * (K ** -0.5)
    k, v, g, beta = blk(k), blk(v), blk(g), blk(beta)
    cum = jnp.cumsum(g, axis=-1)
    seg = cum[..., :, None] - cum[..., None, :]
    incl = jnp.tril(jnp.ones((Q, Q), bool))
    strict = jnp.tril(jnp.ones((Q, Q), bool), -1)
    kb = k * beta[..., None]
    a_mat = jnp.einsum('bchid,bchjd->bchij', kb, k) * jnp.exp(jnp.where(strict, seg, -jnp.inf))
    rhs = jnp.concatenate([v * beta[..., None], kb * jnp.exp(cum)[..., None]], axis=-1)
    sol = lax.linalg.triangular_solve(a_mat, rhs, left_side=True, lower=True, unit_diagonal=True)
    u_base, k_cd = sol[..., :V], sol[..., V:]
    attn = jnp.einsum('bchid,bchjd->bchij', q, k) * jnp.exp(jnp.where(incl, seg, -jnp.inf))
    q_dec = q * jnp.exp(cum)[..., None]
    k_dec = k * jnp.exp(cum[..., -1:] - cum)[..., None]
    end_dec = jnp.exp(cum[..., -1])

    def step(S, inp):
        ub, kcd, at, qd, kd, ed = inp
        u = ub - jnp.einsum('bhqk,bhkv->bhqv', kcd, S)
        o = jnp.einsum('bhqk,bhkv->bhqv', qd, S) + jnp.einsum('bhij,bhjv->bhiv', at, u)
        S = S * ed[..., None, None] + jnp.einsum('bhqk,bhqv->bhkv', kd, u)
        return S, o

    xs = (jnp.moveaxis(u_base, 1, 0), jnp.moveaxis(k_cd, 1, 0), jnp.moveaxis(attn, 1, 0),
          jnp.moveaxis(q_dec, 1, 0), jnp.moveaxis(k_dec, 1, 0), jnp.moveaxis(end_dec, 1, 0))
    s_end, o = lax.scan(step, s0.astype(F32), xs)
    o = jnp.moveaxis(jnp.moveaxis(o, 0, 1), 2, 3).reshape(bsz, L, H, V)
    return o, s_end


def _sgu(u, v, ln_g, ln_b, w_s, b_s):
    bsz, L, _ = u.shape
    Q = min(GMLP_CHUNK, L)
    nc = L // Q
    vg = v.astype(F32).reshape(bsz, L, GMLP_GROUPS, GMLP_GROUP_DIM)
    mu = jnp.mean(vg, axis=-1, keepdims=True)
    var = jnp.mean(jnp.square(vg - mu), axis=-1, keepdims=True)
    vn = (vg - mu) * lax.rsqrt(var + EPS) * ln_g.reshape(GMLP_GROUPS, GMLP_GROUP_DIM) + ln_b.reshape(GMLP_GROUPS, GMLP_GROUP_DIM)
    ws = jnp.where(jnp.tril(jnp.ones((Q, Q), bool)), w_s[:, :Q, :Q].astype(F32), 0.0)
    mixed = jnp.einsum('gij,bnjgc->bnigc', ws, vn.reshape(bsz, nc, Q, GMLP_GROUPS, GMLP_GROUP_DIM))
    mixed = mixed + b_s[:, :Q].astype(F32).T[None, None, :, :, None]
    out = u.astype(F32) * mixed.reshape(bsz, L, GMLP_D)
    return out, vn.reshape(bsz, L, GMLP_D)


def _mixer(h, l, p, ssd_h0, ssd_buf0, gdn_s0, gdn_buf0):
    bsz, L, _ = h.shape
    cuts, acc = [], 0
    for s in IN_SIZES[:-1]:
        acc += s
        cuts.append(acc)
    xbc, z, dt, u, v, qkv, ga, gb, gg = jnp.split(h @ p['w_in'][l], cuts, axis=-1)
    xbc, ssd_buf = _causal_conv(xbc, ssd_buf0, p['ssd_conv_w'][l], p['ssd_conv_b'][l])
    xbc = jax.nn.silu(xbc)
    xs, bm, cm = jnp.split(xbc, [SSD_D, SSD_D + SSD_GROUPS * SSD_STATE], axis=-1)
    dt = jax.nn.softplus(dt.astype(F32) + p['ssd_dt_bias'][l].astype(F32))
    xs_h = xs.reshape(bsz, L, SSD_HEADS, SSD_HEAD_DIM)
    y_a, ssd_h = _ssd(xs_h, dt, p['ssd_a_log'][l], bm.reshape(bsz, L, SSD_GROUPS, SSD_STATE),
                      cm.reshape(bsz, L, SSD_GROUPS, SSD_STATE), ssd_h0)
    y_a = y_a + p['ssd_d'][l].astype(F32)[:, None] * xs_h.astype(F32)
    y_a = y_a.reshape(bsz, L, SSD_D) * jax.nn.silu(z.astype(F32))
    y_a = _rms(y_a.reshape(bsz, L, SSD_GROUPS, SSD_D // SSD_GROUPS),
               p['ssd_norm_w'][l].reshape(SSD_GROUPS, SSD_D // SSD_GROUPS)).reshape(bsz, L, SSD_D)
    y_b, v_rows = _sgu(jax.nn.gelu(u), jax.nn.gelu(v), p['gmlp_ln_g'][l], p['gmlp_ln_b'][l],
                       p['gmlp_ws'][l], p['gmlp_bs'][l])
    qkv, gdn_buf = _causal_conv(qkv, gdn_buf0, p['gdn_conv_w'][l])
    qkv = jax.nn.silu(qkv)
    q, k, vv = jnp.split(qkv, [GDN_QK_D, 2 * GDN_QK_D], axis=-1)
    q = _l2(q.reshape(bsz, L, GDN_HEADS, GDN_KEY_DIM))
    k = _l2(k.reshape(bsz, L, GDN_HEADS, GDN_KEY_DIM))
    vv = vv.reshape(bsz, L, GDN_HEADS, GDN_VAL_DIM)
    beta = jax.nn.sigmoid(gb.astype(F32))
    g = -jnp.exp(p['gdn_a_log'][l].astype(F32)) * jax.nn.softplus(ga.astype(F32) + p['gdn_dt_bias'][l].astype(F32))
    o, gdn_s = _gdn(q, k, vv, g, beta, gdn_s0)
    o = _rms(o, p['gdn_norm_w'][l]) * jax.nn.silu(gg.astype(F32).reshape(bsz, L, GDN_HEADS, GDN_VAL_DIM))
    y_c = o.reshape(bsz, L, GDN_V_D)
    mix = jnp.concatenate([y_a, y_b, y_c], axis=-1).astype(h.dtype)
    new_state = (ssd_h.astype(ssd_h0.dtype), ssd_buf, gdn_s.astype(gdn_s0.dtype), gdn_buf, v_rows.astype(h.dtype))
    return mix @ p['w_out'][l], new_state


def _moe(h, w_router, b_router, w_gate, w_up, w_down):
    bsz, L, D = h.shape
    xt = h.reshape(-1, D)
    n = xt.shape[0]
    aff = jax.nn.sigmoid((xt @ w_router).astype(F32))
    sel = (aff + b_router.astype(F32)).reshape(n, N_EXPERT_GROUPS, EXPERTS_PER_GROUP)
    grp_score = jnp.sum(lax.top_k(sel, TOP_K)[0], axis=-1)
    best = jnp.argmax(grp_score, axis=-1)
    in_grp = (jnp.arange(N_EXPERT_GROUPS)[None, :] == best[:, None])[:, :, None]
    _, idx = lax.top_k(jnp.where(in_grp, sel, -jnp.inf).reshape(n, N_EXPERTS), TOP_K)
    wts = jnp.take_along_axis(aff, idx, axis=-1)
    wts = wts / jnp.sum(wts, axis=-1, keepdims=True)
    flat_e = idx.reshape(-1)
    flat_w = wts.reshape(-1)
    flat_tok = jnp.arange(n * TOP_K, dtype=jnp.int32) // TOP_K
    order = jnp.argsort(flat_e)
    se = flat_e[order]
    counts = jnp.bincount(flat_e, length=N_EXPERTS)
    padded = (counts + MOE_BLOCK - 1) // MOE_BLOCK * MOE_BLOCK
    pad_end = jnp.cumsum(padded)
    pad_start = pad_end - padded
    start = jnp.cumsum(counts) - counts
    dest = pad_start[se] + jnp.arange(n * TOP_K) - start[se]
    n_rows = -(-(n * TOP_K) // MOE_BLOCK) * MOE_BLOCK + N_EXPERTS * MOE_BLOCK
    n_blk = n_rows // MOE_BLOCK
    row_tok = jnp.full((n_rows,), n, jnp.int32).at[dest].set(flat_tok[order])
    row_w = jnp.zeros((n_rows,), F32).at[dest].set(flat_w[order])
    xpad = jnp.concatenate([xt, jnp.zeros((1, D), xt.dtype)], axis=0)
    xb = xpad[row_tok].reshape(n_blk, MOE_BLOCK, D)
    blk_e = jnp.minimum(jnp.sum(jnp.arange(n_blk)[:, None] * MOE_BLOCK >= pad_end[None, :], axis=1), N_EXPERTS - 1)

    def expert_block(args):
        xblk, e = args
        hid = jax.nn.silu(xblk @ w_gate[e]) * (xblk @ w_up[e])
        return hid @ w_down[e]

    yb = lax.map(expert_block, (xb, blk_e)).reshape(n_rows, D)
    y = jnp.zeros((n + 1, D), F32).at[row_tok].add(yb.astype(F32) * row_w[:, None])
    return y[:n].astype(h.dtype).reshape(bsz, L, D)


def _trunk(x, c, ssd_h0, ssd_buf0, gdn_s0, gdn_buf0, p):
    c_act = jax.nn.silu(c)
    outs = ([], [], [], [], [])
    for l in range(DEPTH):
        mod = (c_act @ p['w_ada'][l] + p['b_ada'][l])[:, None, :]
        sh1, sc1, ga1, sh2, sc2, ga2 = jnp.split(mod, 6, axis=-1)
        h = _rms(x, p['norm_mix'][l]) * (1 + sc1) + sh1
        m, st = _mixer(h, l, p, ssd_h0[l], ssd_buf0[l], gdn_s0[l], gdn_buf0[l])
        x = x + ga1 * m
        h = _rms(x, p['norm_ffn'][l]) * (1 + sc2) + sh2
        x = x + ga2 * _moe(h, p['w_router'], p['b_router'], p['w_gate'][l], p['w_up'][l], p['w_down'][l])
        for lst, s in zip(outs, st):
            lst.append(s)
    y = _rms(x, p['norm_final'])
    return y, jnp.stack(outs[0]), jnp.stack(outs[1]), jnp.stack(outs[2]), jnp.stack(outs[3]), jnp.stack(outs[4])


def setup_inputs(seed: int = 0) -> dict:
    key = jax.random.key(seed)
    ks = iter(jax.random.split(key, 48))

    def nrm(shape, scale):
        return jax.random.normal(next(ks), shape, F32) * scale

    def unif(shape, lo, hi):
        return jax.random.uniform(next(ks), shape, F32, lo, hi)

    def dt_bias(shape):
        dt = jnp.exp(unif(shape, math.log(1e-3), math.log(1e-1)))
        return dt + jnp.log(-jnp.expm1(-dt))

    D = D_MODEL
    return {
        'x_prompt': nrm((BATCH, SEQ, D), 1.0),
        'x_sample': nrm((DEC_BATCH, DEC_SEQ, D), 1.0),
        'state_ssd': nrm((DEPTH, DEC_BATCH, SSD_HEADS, SSD_HEAD_DIM, SSD_STATE), 0.1),
        'state_ssd_conv': nrm((DEPTH, DEC_BATCH, CONV_W - 1, SSD_CONV_DIM), 1.0),
        'state_gdn': nrm((DEPTH, DEC_BATCH, GDN_HEADS, GDN_KEY_DIM, GDN_VAL_DIM), 0.3),
        'state_gdn_conv': nrm((DEPTH, DEC_BATCH, CONV_W - 1, GDN_CONV_DIM), 1.0),
        'c_prompt': nrm((BATCH, D), 1.0),
        'c_sample': nrm((DEC_BATCH, D), 1.0),
        'w_ada': nrm((DEPTH, D, 6 * D), 0.5 * D ** -0.5),
        'b_ada': nrm((DEPTH, 6 * D), 0.02),
        'norm_mix': 1.0 + nrm((DEPTH, D), 0.02),
        'norm_ffn': 1.0 + nrm((DEPTH, D), 0.02),
        'norm_final': 1.0 + nrm((D,), 0.02),
        'w_in': nrm((DEPTH, D, IN_D), D ** -0.5),
        'w_out': nrm((DEPTH, MIX_D, D), MIX_D ** -0.5),
        'ssd_conv_w': nrm((DEPTH, CONV_W, SSD_CONV_DIM), CONV_W ** -0.5),
        'ssd_conv_b': nrm((DEPTH, SSD_CONV_DIM), 0.02),
        'ssd_dt_bias': dt_bias((DEPTH, SSD_HEADS)),
        'ssd_a_log': jnp.log(unif((DEPTH, SSD_HEADS), 1.0, 16.0)),
        'ssd_d': 1.0 + nrm((DEPTH, SSD_HEADS), 0.02),
        'ssd_norm_w': 1.0 + nrm((DEPTH, SSD_D), 0.02),
        'gmlp_ln_g': 1.0 + nrm((DEPTH, GMLP_D), 0.02),
        'gmlp_ln_b': nrm((DEPTH, GMLP_D), 0.02),
        'gmlp_ws': nrm((DEPTH, GMLP_GROUPS, GMLP_CHUNK, GMLP_CHUNK), 0.5 * GMLP_CHUNK ** -0.5),
        'gmlp_bs': 1.0 + nrm((DEPTH, GMLP_GROUPS, GMLP_CHUNK), 0.02),
        'gdn_conv_w': nrm((DEPTH, CONV_W, GDN_CONV_DIM), CONV_W ** -0.5),
        'gdn_a_log': jnp.log(unif((DEPTH, GDN_HEADS), 1.0, 16.0)),
        'gdn_dt_bias': dt_bias((DEPTH, GDN_HEADS)),
        'gdn_norm_w': 1.0 + nrm((DEPTH, GDN_VAL_DIM), 0.02),
        'w_router': nrm((D, N_EXPERTS), D ** -0.5),
        'b_router': nrm((N_EXPERTS,), 0.01),
        'w_gate': nrm((DEPTH, N_EXPERTS, D, D_EXPERT), D ** -0.5),
        'w_up': nrm((DEPTH, N_EXPERTS, D, D_EXPERT), D ** -0.5),
        'w_down': nrm((DEPTH, N_EXPERTS, D_EXPERT, D), D_EXPERT ** -0.5),
    }


def reference(x_prompt, x_sample, state_ssd, state_ssd_conv, state_gdn, state_gdn_conv, c_prompt, c_sample,
              w_ada, b_ada, norm_mix, norm_ffn, norm_final, w_in, w_out,
              ssd_conv_w, ssd_conv_b, ssd_dt_bias, ssd_a_log, ssd_d, ssd_norm_w,
              gmlp_ln_g, gmlp_ln_b, gmlp_ws, gmlp_bs,
              gdn_conv_w, gdn_a_log, gdn_dt_bias, gdn_norm_w,
              w_router, b_router, w_gate, w_up, w_down):
    p = dict(w_ada=w_ada, b_ada=b_ada, norm_mix=norm_mix, norm_ffn=norm_ffn, norm_final=norm_final,
             w_in=w_in, w_out=w_out, ssd_conv_w=ssd_conv_w, ssd_conv_b=ssd_conv_b, ssd_dt_bias=ssd_dt_bias,
             ssd_a_log=ssd_a_log, ssd_d=ssd_d, ssd_norm_w=ssd_norm_w, gmlp_ln_g=gmlp_ln_g, gmlp_ln_b=gmlp_ln_b,
             gmlp_ws=gmlp_ws, gmlp_bs=gmlp_bs, gdn_conv_w=gdn_conv_w, gdn_a_log=gdn_a_log,
             gdn_dt_bias=gdn_dt_bias, gdn_norm_w=gdn_norm_w, w_router=w_router, b_router=b_router,
             w_gate=w_gate, w_up=w_up, w_down=w_down)
    bp = x_prompt.shape[0]
    z_ssd = jnp.zeros((DEPTH, bp, SSD_HEADS, SSD_HEAD_DIM, SSD_STATE), state_ssd.dtype)
    z_ssd_conv = jnp.zeros((DEPTH, bp, CONV_W - 1, SSD_CONV_DIM), x_prompt.dtype)
    z_gdn = jnp.zeros((DEPTH, bp, GDN_HEADS, GDN_KEY_DIM, GDN_VAL_DIM), state_gdn.dtype)
    z_gdn_conv = jnp.zeros((DEPTH, bp, CONV_W - 1, GDN_CONV_DIM), x_prompt.dtype)
    y_prompt, p_ssd, p_ssd_conv, p_gdn, p_gdn_conv, _ = _trunk(
        x_prompt, c_prompt, z_ssd, z_ssd_conv, z_gdn, z_gdn_conv, p)
    y_sample, s_ssd, s_ssd_conv, s_gdn, s_gdn_conv, s_gmlp_v = _trunk(
        x_sample, c_sample, state_ssd, state_ssd_conv, state_gdn, state_gdn_conv, p)
    return (y_prompt, y_sample, p_ssd, p_ssd_conv, p_gdn, p_gdn_conv,
            s_ssd, s_ssd_conv, s_gdn, s_gdn_conv, s_gmlp_v)
```

```python
import functools

import jax
import jax.numpy as jnp
from jax import lax
from jax.experimental import pallas as pl
from jax.experimental.pallas import tpu as pltpu

F32 = jnp.float32
BF16 = jnp.bfloat16
I32 = jnp.int32

LANES = 128
SUBLANES = 8
VMEM_LIMIT_BYTES = 56 * 1024 * 1024

D_MODEL = 1024
EPS = 1e-6
CHUNK = 64
CONV_W = 4
SSD_HEADS = 6
SSD_HEAD_DIM = 64
SSD_STATE = 64
SSD_GROUPS = 2
SSD_D = SSD_HEADS * SSD_HEAD_DIM
SSD_CONV_DIM = SSD_D + 2 * SSD_GROUPS * SSD_STATE
GMLP_GROUPS = 4
GMLP_GROUP_DIM = 64
GMLP_D = GMLP_GROUPS * GMLP_GROUP_DIM
GMLP_CHUNK = 128
GDN_HEADS = 6
GDN_KEY_DIM = 64
GDN_VAL_DIM = 64
GDN_QK_D = GDN_HEADS * GDN_KEY_DIM
GDN_V_D = GDN_HEADS * GDN_VAL_DIM
GDN_CONV_DIM = 2 * GDN_QK_D + GDN_V_D
N_EXPERTS = 16
N_EXPERT_GROUPS = 4
EXPERTS_PER_GROUP = N_EXPERTS // N_EXPERT_GROUPS
D_EXPERT = 512

SEG_WIDTHS = (SSD_CONV_DIM, SSD_D, GMLP_D, GMLP_D, GDN_CONV_DIM, GDN_V_D, LANES)
IN_R = sum(SEG_WIDTHS)
SSD_DT_LANE = 0
GDN_A_LANE = 8
GDN_B_LANE = 16
TAIL_ROW = SUBLANES - (CONV_W - 1)

MOE_BLOCK = 256
ROW_TILE = 256
POS_TILE = 2048
DMA_TILE = 256


def _cparams(*sem):
    return pltpu.CompilerParams(dimension_semantics=sem, vmem_limit_bytes=VMEM_LIMIT_BYTES)


def _bdot(a, b):
    return jnp.dot(a.astype(BF16), b.astype(BF16), preferred_element_type=F32)


def _bdot_nt(a, b):
    return lax.dot_general(a.astype(BF16), b.astype(BF16), (((1,), (1,)), ((), ())), preferred_element_type=F32)


def _bdot_tn(a, b):
    return lax.dot_general(a.astype(BF16), b.astype(BF16), (((0,), (0,)), ((), ())), preferred_element_type=F32)


def _sigmoid(x):
    return 1.0 / (1.0 + jnp.exp(-x))


def _silu(x):
    return x * _sigmoid(x)


def _softplus(x):
    return jnp.maximum(x, 0.0) + jnp.log1p(jnp.exp(-jnp.abs(x)))


def _gelu_tanh(x):
    return 0.5 * x * (1.0 + jnp.tanh(0.7978845608028654 * (x + 0.044715 * (x * x * x))))


def _cumsum_rows(a):
    n = a.shape[0]
    row = lax.broadcasted_iota(I32, a.shape, 0)
    s = 1
    while s < n:
        a = a + jnp.where(row >= s, pltpu.roll(a, s, axis=0), jnp.zeros_like(a))
        s *= 2
    return a


def _cumsum_lanes(a):
    n = a.shape[1]
    col = lax.broadcasted_iota(I32, a.shape, 1)
    s = 1
    while s < n:
        a = a + jnp.where(col >= s, pltpu.roll(a, s, axis=1), jnp.zeros_like(a))
        s *= 2
    return a


def _rows_to_lanes(a):
    q = a.shape[0]
    if q < LANES:
        a = jnp.concatenate([a, jnp.zeros((LANES - q, LANES), a.dtype)], axis=0)
    return a.T[:, :q]


def _ada_body(c_ref, w_ref, b_ref, o_ref):
    o_ref[0] = _bdot(_silu(c_ref[...]), w_ref[0]) + b_ref[0]


def _ada_mod(c_all, w_ada, b_ada):
    depth, d, n6 = w_ada.shape
    bc = c_all.shape[0]
    tn = 1536
    return pl.pallas_call(
        _ada_body,
        grid=(depth, n6 // tn),
        in_specs=[pl.BlockSpec((bc, d), lambda l, j: (0, 0)),
                  pl.BlockSpec((1, d, tn), lambda l, j: (l, 0, j)),
                  pl.BlockSpec((1, 1, tn), lambda l, j: (l, 0, j))],
        out_specs=pl.BlockSpec((1, bc, tn), lambda l, j: (l, 0, j)),
        out_shape=jax.ShapeDtypeStruct((depth, bc, n6), F32),
        compiler_params=_cparams("parallel", "parallel"),
        name="ada_mod",
    )(c_all, w_ada, b_ada.reshape(depth, 1, n6))


def _seq_block(tm, seq_len, d):
    if tm <= seq_len:
        per_seq = seq_len // tm
        return pl.BlockSpec((1, 1, d), lambda i: (i // per_seq, 0, 0))
    return pl.BlockSpec((tm // seq_len, 1, d), lambda i: (i, 0, 0))


def _seq_rows(ref, tm):
    s = ref.shape[0]
    if s == 1:
        return ref[0]
    return jnp.concatenate([jnp.broadcast_to(ref[k], (tm // s, ref.shape[2])) for k in range(s)], axis=0)


def _mod_rms(x, nw, sc, sh):
    ms = jnp.mean(x * x, axis=-1, keepdims=True)
    return (x * lax.rsqrt(ms + EPS) * nw) * (1.0 + sc) + sh


def _inproj_body(x_ref, sh_ref, sc_ref, nw_ref, w_ref, *out_refs):
    tm = x_ref.shape[0]
    hb = _mod_rms(x_ref[...], nw_ref[...], _seq_rows(sc_ref, tm), _seq_rows(sh_ref, tm)).astype(BF16)
    off = 0
    for ref, width in zip(out_refs, SEG_WIDTHS):
        ref[...] = jnp.dot(hb, w_ref[:, off:off + width], preferred_element_type=F32)
        off += width


def _inproj(x2, sh, sc, nw, w_r, seq_len):
    n, d = x2.shape
    tm = min(ROW_TILE, n)
    row = lambda i: (i, 0)
    return pl.pallas_call(
        _inproj_body,
        grid=(n // tm,),
        in_specs=[pl.BlockSpec((tm, d), row),
                  _seq_block(tm, seq_len, d),
                  _seq_block(tm, seq_len, d),
                  pl.BlockSpec((1, d), lambda i: (0, 0)),
                  pl.BlockSpec((d, IN_R), lambda i: (0, 0))],
        out_specs=[pl.BlockSpec((tm, w), row) for w in SEG_WIDTHS],
        out_shape=[jax.ShapeDtypeStruct((n, w), F32) for w in SEG_WIDTHS],
        compiler_params=_cparams("parallel"),
        name="inproj",
    )(x2, sh, sc, nw, w_r)


def _causal_conv_step(x_ref, buf0_ref, bufout_ref, xp_ref, cw_ref, first):
    q = x_ref.shape[0]

    @pl.when(first)
    def _():
        xp_ref[TAIL_ROW:SUBLANES, :] = buf0_ref[0]

    xp_ref[SUBLANES:SUBLANES + q, :] = x_ref[...]
    acc = cw_ref[0:1, :] * xp_ref[TAIL_ROW:TAIL_ROW + q, :]
    for k in range(1, CONV_W):
        acc = acc + cw_ref[k:k + 1, :] * xp_ref[TAIL_ROW + k:TAIL_ROW + k + q, :]
    tail = xp_ref[TAIL_ROW + q:SUBLANES + q, :]
    bufout_ref[0] = tail
    xp_ref[TAIL_ROW:SUBLANES, :] = tail
    return acc


def _decay_terms(g):
    q = g.shape[0]
    cum = _cumsum_rows(g)
    cum_t = _rows_to_lanes(cum)
    last = cum[q - 1:q, :]
    return cum, cum_t, jnp.exp(cum), jnp.exp(last - cum), jnp.exp(last)


def _ssd_body(xbc_ref, z_ref, sm_ref, h0_ref, buf0_ref, cw_ref, cb_ref, dtb_ref, alog_ref, dvec_ref, nw_ref,
              y_ref, hout_ref, bufout_ref, xp_ref, ht_ref, ys_ref):
    q = xbc_ref.shape[0]
    first = pl.program_id(1) == 0

    @pl.when(first)
    def _():
        ht_ref[...] = h0_ref[0]

    xa = _silu(_causal_conv_step(xbc_ref, buf0_ref, bufout_ref, xp_ref, cw_ref, first) + cb_ref[...])
    xs = xa[:, :SSD_D]
    bm = xa[:, SSD_D:SSD_D + SSD_GROUPS * SSD_STATE]
    cm = xa[:, SSD_D + SSD_GROUPS * SSD_STATE:]
    dt = _softplus(sm_ref[...] + dtb_ref[...])
    cum, cum_t, ecum, wend, edec = _decay_terms(dt * (-jnp.exp(alog_ref[...])))
    causal = lax.broadcasted_iota(I32, (q, q), 0) >= lax.broadcasted_iota(I32, (q, q), 1)
    heads_per_group = SSD_HEADS // SSD_GROUPS
    for g in range(SSD_GROUPS):
        bg = bm[:, g * SSD_STATE:(g + 1) * SSD_STATE]
        cg = cm[:, g * SSD_STATE:(g + 1) * SSD_STATE]
        cb = _bdot_nt(cg, bg)
        for h in range(g * heads_per_group, (g + 1) * heads_per_group):
            lane = SSD_DT_LANE + h
            hs = slice(h * SSD_HEAD_DIM, (h + 1) * SSD_HEAD_DIM)
            seg = cum[:, lane:lane + 1] - cum_t[lane:lane + 1, :]
            scores = cb * jnp.where(causal, jnp.exp(seg), 0.0)
            xh = xs[:, hs]
            xdt = xh * dt[:, lane:lane + 1]
            ht = ht_ref[h]
            y = _bdot(scores, xdt) + _bdot(cg * ecum[:, lane:lane + 1], ht)
            ys_ref[:, hs] = y + dvec_ref[:, hs] * xh
            ht_ref[h] = ht * edec[:, lane:lane + 1] + _bdot_tn(bg * wend[:, lane:lane + 1], xdt)
    y = ys_ref[...] * _silu(z_ref[...])
    half = SSD_D // SSD_GROUPS
    in_g0 = lax.broadcasted_iota(I32, y.shape, 1) < half
    y2 = y * y
    ss0 = jnp.sum(jnp.where(in_g0, y2, 0.0), axis=-1, keepdims=True)
    ss1 = jnp.sum(jnp.where(in_g0, 0.0, y2), axis=-1, keepdims=True)
    scale = jnp.where(in_g0, lax.rsqrt(ss0 / half + EPS), lax.rsqrt(ss1 / half + EPS))
    y_ref[...] = y * scale * nw_ref[...]
    hout_ref[0] = ht_ref[...]


def _ssd(xbc, z, sm, h0t, buf0, cw, cb, dtb, alog, dvec, nw, bsz, seq_len):
    q = min(CHUNK, seq_len)
    nc = seq_len // q
    n = bsz * seq_len
    tok = lambda b, c: (b * nc + c, 0)
    per_b4 = lambda b, c: (b, 0, 0, 0)
    per_b3 = lambda b, c: (b, 0, 0)
    const = lambda b, c: (0, 0)
    return pl.pallas_call(
        _ssd_body,
        grid=(bsz, nc),
        in_specs=[pl.BlockSpec((q, SSD_CONV_DIM), tok),
                  pl.BlockSpec((q, SSD_D), tok),
                  pl.BlockSpec((q, LANES), tok),
                  pl.BlockSpec((1, SSD_HEADS, SSD_STATE, SSD_HEAD_DIM), per_b4),
                  pl.BlockSpec((1, CONV_W - 1, SSD_CONV_DIM), per_b3),
                  pl.BlockSpec((CONV_W, SSD_CONV_DIM), const),
                  pl.BlockSpec((1, SSD_CONV_DIM), const),
                  pl.BlockSpec((1, LANES), const),
                  pl.BlockSpec((1, LANES), const),
                  pl.BlockSpec((1, SSD_D), const),
                  pl.BlockSpec((1, SSD_D), const)],
        out_specs=[pl.BlockSpec((q, SSD_D), tok),
                   pl.BlockSpec((1, SSD_HEADS, SSD_STATE, SSD_HEAD_DIM), per_b4),
                   pl.BlockSpec((1, CONV_W - 1, SSD_CONV_DIM), per_b3)],
        out_shape=[jax.ShapeDtypeStruct((n, SSD_D), F32),
                   jax.ShapeDtypeStruct((bsz, SSD_HEADS, SSD_STATE, SSD_HEAD_DIM), F32),
                   jax.ShapeDtypeStruct((bsz, CONV_W - 1, SSD_CONV_DIM), F32)],
        scratch_shapes=[pltpu.VMEM((SUBLANES + q, SSD_CONV_DIM), F32),
                        pltpu.VMEM((SSD_HEADS, SSD_STATE, SSD_HEAD_DIM), F32),
                        pltpu.VMEM((q, SSD_D), F32)],
        compiler_params=_cparams("parallel", "arbitrary"),
        name="ssd",
    )(xbc, z, sm, h0t, buf0, cw, cb, dtb, alog, dvec, nw)


def _gmlp_body(u_ref, v_ref, lng_ref, lnb_ref, ws_ref, bst_ref, y_ref, vn_ref):
    q = u_ref.shape[0]
    gu = _gelu_tanh(u_ref[...])
    gv = _gelu_tanh(v_ref[...])
    lower = lax.broadcasted_iota(I32, (q, q), 0) >= lax.broadcasted_iota(I32, (q, q), 1)
    for g in range(GMLP_GROUPS):
        gs = slice(g * GMLP_GROUP_DIM, (g + 1) * GMLP_GROUP_DIM)
        vg = gv[:, gs]
        mu = jnp.mean(vg, axis=-1, keepdims=True)
        dv = vg - mu
        var = jnp.mean(dv * dv, axis=-1, keepdims=True)
        vn = dv * lax.rsqrt(var + EPS) * lng_ref[:, gs] + lnb_ref[:, gs]
        vn_ref[:, gs] = vn
        mixed = _bdot(jnp.where(lower, ws_ref[g], 0.0), vn) + bst_ref[:, g:g + 1]
        y_ref[:, gs] = gu[:, gs] * mixed


def _gmlp(u, v, lng, lnb, ws, bst, bsz, seq_len):
    q = min(GMLP_CHUNK, seq_len)
    n = bsz * seq_len
    tok = lambda i: (i, 0)
    return pl.pallas_call(
        _gmlp_body,
        grid=(n // q,),
        in_specs=[pl.BlockSpec((q, GMLP_D), tok),
                  pl.BlockSpec((q, GMLP_D), tok),
                  pl.BlockSpec((1, GMLP_D), lambda i: (0, 0)),
                  pl.BlockSpec((1, GMLP_D), lambda i: (0, 0)),
                  pl.BlockSpec((GMLP_GROUPS, q, q), lambda i: (0, 0, 0)),
                  pl.BlockSpec((q, GMLP_GROUPS), lambda i: (0, 0))],
        out_specs=[pl.BlockSpec((q, GMLP_D), tok), pl.BlockSpec((q, GMLP_D), tok)],
        out_shape=[jax.ShapeDtypeStruct((n, GMLP_D), F32), jax.ShapeDtypeStruct((n, GMLP_D), F32)],
        compiler_params=_cparams("parallel"),
        name="gmlp",
    )(u, v, lng, lnb, ws, bst)


def _l2_rows(x):
    return x * lax.rsqrt(jnp.sum(x * x, axis=-1, keepdims=True) + EPS)


def _unit_lower_inverse_minus_identity(a):
    q = a.shape[0]
    ii = lax.broadcasted_iota(I32, (q, q), 0)
    jj = lax.broadcasted_iota(I32, (q, q), 1)
    r = -jnp.where((ii >> 1) == (jj >> 1), a, 0.0)
    level = 1
    while (1 << level) < q:
        in_pair = (ii >> (level + 1)) == (jj >> (level + 1))
        off_diag = (ii >> level) != (jj >> level)
        am = jnp.where(jnp.logical_and(in_pair, off_diag), a, 0.0)
        w = am + _bdot(am, r)
        r = r - w - _bdot(r, w)
        level += 1
    return r


def _gdn_body(qkv_ref, gg_ref, sm_ref, s0_ref, buf0_ref, cw_ref, dtb_ref, alog_ref, nw_ref,
              y_ref, sout_ref, bufout_ref, xp_ref, s_ref, ys_ref):
    q = qkv_ref.shape[0]
    first = pl.program_id(1) == 0

    @pl.when(first)
    def _():
        s_ref[...] = s0_ref[0]

    xa = _silu(_causal_conv_step(qkv_ref, buf0_ref, bufout_ref, xp_ref, cw_ref, first))
    sm = sm_ref[...]
    beta_all = _sigmoid(sm)
    g_all = -jnp.exp(alog_ref[...]) * _softplus(sm + dtb_ref[...])
    cum, cum_t, ecum, kdecw, edec = _decay_terms(g_all)
    ii = lax.broadcasted_iota(I32, (q, q), 0)
    jj = lax.broadcasted_iota(I32, (q, q), 1)
    gg = gg_ref[...]
    for h in range(GDN_HEADS):
        lane = GDN_A_LANE + h
        ks = slice(h * GDN_KEY_DIM, (h + 1) * GDN_KEY_DIM)
        vs = slice(h * GDN_VAL_DIM, (h + 1) * GDN_VAL_DIM)
        qh = _l2_rows(xa[:, ks]) * (GDN_KEY_DIM ** -0.5)
        kh = _l2_rows(xa[:, GDN_QK_D + h * GDN_KEY_DIM:GDN_QK_D + (h + 1) * GDN_KEY_DIM])
        vh = xa[:, 2 * GDN_QK_D + h * GDN_VAL_DIM:2 * GDN_QK_D + (h + 1) * GDN_VAL_DIM]
        beta = beta_all[:, GDN_B_LANE + h:GDN_B_LANE + h + 1]
        kb = kh * beta
        e = jnp.exp(cum[:, lane:lane + 1] - cum_t[lane:lane + 1, :])
        a_mat = _bdot_nt(kb, kh) * jnp.where(ii > jj, e, 0.0)
        attn = _bdot_nt(qh, kh) * jnp.where(ii >= jj, e, 0.0)
        r = _unit_lower_inverse_minus_identity(a_mat)
        vb = vh * beta
        kbe = kb * ecum[:, lane:lane + 1]
        u_base = vb + _bdot(r, vb)
        k_cd = kbe + _bdot(r, kbe)
        s = s_ref[h]
        u = u_base - _bdot(k_cd, s)
        o = _bdot(qh * ecum[:, lane:lane + 1], s) + _bdot(attn, u)
        s_ref[h] = s * edec[:, lane:lane + 1] + _bdot_tn(kh * kdecw[:, lane:lane + 1], u)
        o = o * lax.rsqrt(jnp.mean(o * o, axis=-1, keepdims=True) + EPS) * nw_ref[...]
        ys_ref[:, vs] = o * _silu(gg[:, vs])
    y_ref[...] = ys_ref[...]
    sout_ref[0] = s_ref[...]


def _gdn(qkv, gg, sm, s0, buf0, cw, dtb, alog, nw, bsz, seq_len):
    q = min(CHUNK, seq_len)
    nc = seq_len // q
    n = bsz * seq_len
    tok = lambda b, c: (b * nc + c, 0)
    per_b4 = lambda b, c: (b, 0, 0, 0)
    per_b3 = lambda b, c: (b, 0, 0)
    const = lambda b, c: (0, 0)
    return pl.pallas_call(
        _gdn_body,
        grid=(bsz, nc),
        in_specs=[pl.BlockSpec((q, GDN_CONV_DIM), tok),
                  pl.BlockSpec((q, GDN_V_D), tok),
                  pl.BlockSpec((q, LANES), tok),
                  pl.BlockSpec((1, GDN_HEADS, GDN_KEY_DIM, GDN_VAL_DIM), per_b4),
                  pl.BlockSpec((1, CONV_W - 1, GDN_CONV_DIM), per_b3),
                  pl.BlockSpec((CONV_W, GDN_CONV_DIM), const),
                  pl.BlockSpec((1, LANES), const),
                  pl.BlockSpec((1, LANES), const),
                  pl.BlockSpec((1, GDN_VAL_DIM), const)],
        out_specs=[pl.BlockSpec((q, GDN_V_D), tok),
                   pl.BlockSpec((1, GDN_HEADS, GDN_KEY_DIM, GDN_VAL_DIM), per_b4),
                   pl.BlockSpec((1, CONV_W - 1, GDN_CONV_DIM), per_b3)],
        out_shape=[jax.ShapeDtypeStruct((n, GDN_V_D), F32),
                   jax.ShapeDtypeStruct((bsz, GDN_HEADS, GDN_KEY_DIM, GDN_VAL_DIM), F32),
                   jax.ShapeDtypeStruct((bsz, CONV_W - 1, GDN_CONV_DIM), F32)],
        scratch_shapes=[pltpu.VMEM((SUBLANES + q, GDN_CONV_DIM), F32),
                        pltpu.VMEM((GDN_HEADS, GDN_KEY_DIM, GDN_VAL_DIM), F32),
                        pltpu.VMEM((q, GDN_V_D), F32)],
        compiler_params=_cparams("parallel", "arbitrary"),
        name="gdn",
    )(qkv, gg, sm, s0, buf0, cw, dtb, alog, nw)


def _route(logits, b_col):
    aff = _sigmoid(logits)
    sel = aff + b_col
    srow = [sel[e:e + 1, :] for e in range(N_EXPERTS)]
    arow = [aff[e:e + 1, :] for e in range(N_EXPERTS)]
    gscore = []
    for g in range(N_EXPERT_GROUPS):
        a, b, c, d = srow[g * EXPERTS_PER_GROUP:(g + 1) * EXPERTS_PER_GROUP]
        hi1, lo1, hi2, lo2 = jnp.maximum(a, b), jnp.minimum(a, b), jnp.maximum(c, d), jnp.minimum(c, d)
        gscore.append(jnp.maximum(hi1, hi2) + jnp.maximum(jnp.minimum(hi1, hi2), jnp.maximum(lo1, lo2)))
    best = jnp.zeros(gscore[0].shape, I32)
    best_v = gscore[0]
    for g in range(1, N_EXPERT_GROUPS):
        upd = gscore[g] > best_v
        best = jnp.where(upd, g, best)
        best_v = jnp.where(upd, gscore[g], best_v)
    cs, ca = [], []
    for j in range(EXPERTS_PER_GROUP):
        cj, aj = srow[j], arow[j]
        for g in range(1, N_EXPERT_GROUPS):
            m = best == g
            cj = jnp.where(m, srow[g * EXPERTS_PER_GROUP + j], cj)
            aj = jnp.where(m, arow[g * EXPERTS_PER_GROUP + j], aj)
        cs.append(cj)
        ca.append(aj)
    i1 = jnp.zeros(best.shape, I32)
    v1, a1 = cs[0], ca[0]
    for j in range(1, EXPERTS_PER_GROUP):
        upd = cs[j] > v1
        i1 = jnp.where(upd, j, i1)
        v1 = jnp.where(upd, cs[j], v1)
        a1 = jnp.where(upd, ca[j], a1)
    i2 = jnp.zeros(best.shape, I32)
    v2 = jnp.full(v1.shape, -jnp.inf, F32)
    a2 = jnp.zeros(v1.shape, F32)
    for j in range(EXPERTS_PER_GROUP):
        upd = jnp.logical_and(i1 != j, cs[j] > v2)
        i2 = jnp.where(upd, j, i2)
        v2 = jnp.where(upd, cs[j], v2)
        a2 = jnp.where(upd, ca[j], a2)
    tot = a1 + a2
    return best * EXPERTS_PER_GROUP + i1, best * EXPERTS_PER_GROUP + i2, a1 / tot, a2 / tot


def _outproj_body(x_ref, ya_ref, yb_ref, yc_ref, wa_ref, wb_ref, wc_ref, ga_ref, nw_ref, sc_ref, sh_ref,
                  wrt_ref, br_ref, xo_ref, h2_ref, eidx_ref, wt_ref):
    m = _bdot(ya_ref[...], wa_ref[...]) + _bdot(yb_ref[...], wb_ref[...]) + _bdot(yc_ref[...], wc_ref[...])
    tm = x_ref.shape[0]
    x = x_ref[...] + _seq_rows(ga_ref, tm) * m
    xo_ref[...] = x
    h2 = _mod_rms(x, nw_ref[...], _seq_rows(sc_ref, tm), _seq_rows(sh_ref, tm))
    h2_ref[...] = h2
    e1, e2, w1, w2 = _route(_bdot_nt(wrt_ref[...], h2), br_ref[...])
    eidx_ref[0:1, :] = e1
    eidx_ref[1:2, :] = e2
    row = lax.broadcasted_iota(I32, (LANES, w1.shape[1]), 0)
    wt_ref[...] = jnp.where(row == 0, w1, jnp.where(row == 1, w2, 0.0)).T


def _outproj_route(x2, ya, yb, yc, wa, wb, wc, ga1, nw, sc2, sh2, wrt, br, seq_len):
    n, d = x2.shape
    tm = min(ROW_TILE, n)
    row = lambda i: (i, 0)
    seq = _seq_block(tm, seq_len, d)
    const = lambda i: (0, 0)
    return pl.pallas_call(
        _outproj_body,
        grid=(n // tm,),
        in_specs=[pl.BlockSpec((tm, d), row),
                  pl.BlockSpec((tm, SSD_D), row),
                  pl.BlockSpec((tm, GMLP_D), row),
                  pl.BlockSpec((tm, GDN_V_D), row),
                  pl.BlockSpec((SSD_D, d), const),
                  pl.BlockSpec((GMLP_D, d), const),
                  pl.BlockSpec((GDN_V_D, d), const),
                  seq,
                  pl.BlockSpec((1, d), const),
                  seq,
                  seq,
                  pl.BlockSpec((N_EXPERTS, d), const),
                  pl.BlockSpec((N_EXPERTS, 1), const)],
        out_specs=[pl.BlockSpec((tm, d), row),
                   pl.BlockSpec((tm, d), row),
                   pl.BlockSpec((2, tm), lambda i: (0, i)),
                   pl.BlockSpec((tm, LANES), row)],
        out_shape=[jax.ShapeDtypeStruct((n, d), F32),
                   jax.ShapeDtypeStruct((n, d), F32),
                   jax.ShapeDtypeStruct((2, n), I32),
                   jax.ShapeDtypeStruct((n, LANES), F32)],
        compiler_params=_cparams("parallel"),
        name="outproj_route",
    )(x2, ya, yb, yc, wa, wb, wc, ga1, nw, sc2, sh2, wrt, br)


def _positions_body(eidx_ref, dest_ref, blk_ref, nblk_ref, cnt_ref, run_ref):
    phase = pl.program_id(0)
    i = pl.program_id(1)
    t = eidx_ref.shape[1]
    e_iota = lax.broadcasted_iota(I32, (N_EXPERTS, t), 0)
    m0 = e_iota == eidx_ref[0:1, :]
    m1 = e_iota == eidx_ref[1:2, :]
    hits = jnp.where(jnp.logical_or(m0, m1), 1.0, 0.0)
    tile_cnt = jnp.broadcast_to(jnp.sum(hits, axis=1, keepdims=True), (N_EXPERTS, LANES))

    @pl.when(jnp.logical_and(phase == 0, i == 0))
    def _():
        cnt_ref[...] = jnp.zeros_like(cnt_ref)

    @pl.when(phase == 0)
    def _():
        cnt_ref[...] += tile_cnt

    @pl.when(jnp.logical_and(phase == 1, i == 0))
    def _():
        padded = jnp.floor((cnt_ref[...] + (MOE_BLOCK - 1)) * (1.0 / MOE_BLOCK)) * MOE_BLOCK
        pad_end = _cumsum_rows(padded)
        run_ref[...] = pad_end - padded
        nb = blk_ref.shape[1]
        blk_start = (lax.broadcasted_iota(I32, (N_EXPERTS, nb), 1) * MOE_BLOCK).astype(F32)
        past = jnp.where(blk_start >= pad_end[:, 0:1], 1.0, 0.0)
        blk_ref[...] = jnp.minimum(jnp.sum(past, axis=0, keepdims=True), N_EXPERTS - 1.0).astype(I32)
        nblk_ref[...] = (pad_end[N_EXPERTS - 1:N_EXPERTS, :] * (1.0 / MOE_BLOCK)).astype(I32)

    @pl.when(phase == 1)
    def _():
        excl = _cumsum_lanes(hits) - hits
        pos = run_ref[:, 0:1] + excl
        dest_ref[0:1, :] = jnp.sum(jnp.where(m0, pos, 0.0), axis=0, keepdims=True).astype(I32)
        dest_ref[1:2, :] = jnp.sum(jnp.where(m1, pos, 0.0), axis=0, keepdims=True).astype(I32)
        run_ref[...] += tile_cnt


def _positions(eidx, n_blocks):
    n = eidx.shape[1]
    t = min(POS_TILE, n)
    nb_pad = -(-n_blocks // LANES) * LANES
    return pl.pallas_call(
        _positions_body,
        grid=(2, n // t),
        in_specs=[pl.BlockSpec((2, t), lambda p, i: (0, i))],
        out_specs=[pl.BlockSpec((2, t), lambda p, i: (0, i * p)),
                   pl.BlockSpec((1, nb_pad), lambda p, i: (0, 0)),
                   pl.BlockSpec((1, LANES), lambda p, i: (0, 0))],
        out_shape=[jax.ShapeDtypeStruct((2, n), I32),
                   jax.ShapeDtypeStruct((1, nb_pad), I32),
                   jax.ShapeDtypeStruct((1, LANES), I32)],
        scratch_shapes=[pltpu.VMEM((N_EXPERTS, LANES), F32), pltpu.VMEM((N_EXPERTS, LANES), F32)],
        compiler_params=_cparams("arbitrary", "arbitrary"),
        name="moe_positions",
    )(eidx)


def _row_copy(src_ref, src_row, dst_ref, dst_row, sem):
    return pltpu.make_async_copy(src_ref.at[pl.ds(src_row, 1), :], dst_ref.at[pl.ds(dst_row, 1), :], sem)


def _dispatch_body(dest_ref, h_ref, xs_in_ref, xs_ref, sem):
    del xs_in_ref
    t = dest_ref.shape[1]
    base = pl.program_id(0) * t

    def issue(j, carry):
        _row_copy(h_ref, base + j, xs_ref, dest_ref[0, j], sem.at[0]).start()
        _row_copy(h_ref, base + j, xs_ref, dest_ref[1, j], sem.at[1]).start()
        return carry

    lax.fori_loop(0, t, issue, 0)

    def drain(j, carry):
        _row_copy(h_ref, base + j, xs_ref, dest_ref[0, j], sem.at[0]).wait()
        _row_copy(h_ref, base + j, xs_ref, dest_ref[1, j], sem.at[1]).wait()
        return carry

    lax.fori_loop(0, t, drain, 0)


def _dispatch(dest, h2, n_rows):
    n, d = h2.shape
    t = min(DMA_TILE, n)
    xs0 = jnp.zeros((n_rows, d), F32)
    return pl.pallas_call(
        _dispatch_body,
        grid=(n // t,),
        in_specs=[pl.BlockSpec((2, t), lambda i: (0, i), memory_space=pltpu.SMEM),
                  pl.BlockSpec(memory_space=pl.ANY),
                  pl.BlockSpec(memory_space=pl.ANY)],
        out_specs=pl.BlockSpec(memory_space=pl.ANY),
        out_shape=jax.ShapeDtypeStruct((n_rows, d), F32),
        scratch_shapes=[pltpu.SemaphoreType.DMA((2,))],
        input_output_aliases={2: 0},
        compiler_params=_cparams("arbitrary"),
        name="moe_dispatch",
    )(dest, h2, xs0)


def _experts_body(blk_ref, nblk_ref, xs_ref, wg_ref, wu_ref, wd_ref, y_ref):
    used = pl.program_id(0) < nblk_ref[0]

    @pl.when(used)
    def _():
        x = xs_ref[...].astype(BF16)
        hid = _silu(jnp.dot(x, wg_ref[0], preferred_element_type=F32)) * jnp.dot(x, wu_ref[0], preferred_element_type=F32)
        y_ref[...] = jnp.dot(hid.astype(BF16), wd_ref[0], preferred_element_type=F32)

    @pl.when(jnp.logical_not(used))
    def _():
        y_ref[...] = jnp.zeros_like(y_ref)


def _experts(blk_e, nblk, xs, wg, wu, wd):
    n_rows, d = xs.shape
    nb = n_rows // MOE_BLOCK

    def live(b, blk, nbl):
        return jnp.minimum(b, nbl[0] - 1)

    rows = lambda b, blk, nbl: (live(b, blk, nbl), 0)
    wsel = lambda b, blk, nbl: (blk[live(b, blk, nbl)], 0, 0)
    return pl.pallas_call(
        _experts_body,
        grid_spec=pltpu.PrefetchScalarGridSpec(
            num_scalar_prefetch=2,
            grid=(nb,),
            in_specs=[pl.BlockSpec((MOE_BLOCK, d), rows),
                      pl.BlockSpec((1, d, D_EXPERT), wsel),
                      pl.BlockSpec((1, d, D_EXPERT), wsel),
                      pl.BlockSpec((1, D_EXPERT, d), wsel)],
            out_specs=pl.BlockSpec((MOE_BLOCK, d), lambda b, blk, nbl: (b, 0))),
        out_shape=jax.ShapeDtypeStruct((n_rows, d), F32),
        compiler_params=_cparams("arbitrary"),
        name="moe_experts",
    )(blk_e, nblk, xs, wg, wu, wd)


def _combine_body(dest_ref, x_ref, wt_ref, ga_ref, fw_ref, yb_ref, o_ref, r0_ref, r1_ref, sem, *, final_norm):
    t = dest_ref.shape[1]

    def issue(j, carry):
        _row_copy(yb_ref, dest_ref[0, j], r0_ref, j, sem.at[0]).start()
        _row_copy(yb_ref, dest_ref[1, j], r1_ref, j, sem.at[1]).start()
        return carry

    lax.fori_loop(0, t, issue, 0)

    def drain(j, carry):
        _row_copy(yb_ref, dest_ref[0, j], r0_ref, j, sem.at[0]).wait()
        _row_copy(yb_ref, dest_ref[1, j], r1_ref, j, sem.at[1]).wait()
        return carry

    lax.fori_loop(0, t, drain, 0)
    wt = wt_ref[...]
    moe = r0_ref[...] * wt[:, 0:1] + r1_ref[...] * wt[:, 1:2]
    x = x_ref[...] + _seq_rows(ga_ref, t) * moe
    if final_norm:
        x = x * lax.rsqrt(jnp.mean(x * x, axis=-1, keepdims=True) + EPS) * fw_ref[...]
    o_ref[...] = x


def _combine(dest, x2, wts_t, ga2, fw, yb, seq_len, final_norm):
    n, d = x2.shape
    t = min(DMA_TILE, n)
    row = lambda i: (i, 0)
    return pl.pallas_call(
        functools.partial(_combine_body, final_norm=final_norm),
        grid=(n // t,),
        in_specs=[pl.BlockSpec((2, t), lambda i: (0, i), memory_space=pltpu.SMEM),
                  pl.BlockSpec((t, d), row),
                  pl.BlockSpec((t, LANES), row),
                  _seq_block(t, seq_len, d),
                  pl.BlockSpec((1, d), lambda i: (0, 0)),
                  pl.BlockSpec(memory_space=pl.ANY)],
        out_specs=pl.BlockSpec((t, d), row),
        out_shape=jax.ShapeDtypeStruct((n, d), F32),
        scratch_shapes=[pltpu.VMEM((t, d), F32), pltpu.VMEM((t, d), F32), pltpu.SemaphoreType.DMA((2,))],
        compiler_params=_cparams("arbitrary"),
        name="moe_combine",
    )(dest, x2, wts_t, ga2, fw, yb)


def _lane_vec(vals, lane):
    return jnp.zeros((1, LANES), F32).at[0, lane:lane + vals.shape[0]].set(vals.astype(F32))


def _layer_params(p, l):
    w_in = p["w_in"][l]
    cuts = [0]
    for s in (SSD_CONV_DIM, SSD_D, SSD_HEADS, GMLP_D, GMLP_D, GDN_CONV_DIM, GDN_HEADS, GDN_HEADS, GDN_V_D):
        cuts.append(cuts[-1] + s)
    xbc, z, dt, u, v, qkv, ga, gb, gg = (w_in[:, a:b] for a, b in zip(cuts[:-1], cuts[1:]))
    small = jnp.zeros((D_MODEL, LANES), F32)
    small = small.at[:, SSD_DT_LANE:SSD_DT_LANE + SSD_HEADS].set(dt)
    small = small.at[:, GDN_A_LANE:GDN_A_LANE + GDN_HEADS].set(ga)
    small = small.at[:, GDN_B_LANE:GDN_B_LANE + GDN_HEADS].set(gb)
    w_out = p["w_out"][l].astype(BF16)
    return dict(
        w_r=jnp.concatenate([xbc, z, u, v, qkv, gg, small], axis=1).astype(BF16),
        wa=w_out[:SSD_D], wb=w_out[SSD_D:SSD_D + GMLP_D], wc=w_out[SSD_D + GMLP_D:],
        norm_mix=p["norm_mix"][l][None, :], norm_ffn=p["norm_ffn"][l][None, :],
        ssd_cw=p["ssd_conv_w"][l], ssd_cb=p["ssd_conv_b"][l][None, :],
        ssd_dtb=_lane_vec(p["ssd_dt_bias"][l], SSD_DT_LANE), ssd_alog=_lane_vec(p["ssd_a_log"][l], SSD_DT_LANE),
        ssd_dvec=jnp.repeat(p["ssd_d"][l].astype(F32), SSD_HEAD_DIM)[None, :], ssd_nw=p["ssd_norm_w"][l][None, :],
        lng=p["gmlp_ln_g"][l][None, :], lnb=p["gmlp_ln_b"][l][None, :], ws=p["gmlp_ws"][l], bs=p["gmlp_bs"][l],
        gdn_cw=p["gdn_conv_w"][l],
        gdn_dtb=_lane_vec(p["gdn_dt_bias"][l], GDN_A_LANE), gdn_alog=_lane_vec(p["gdn_a_log"][l], GDN_A_LANE),
        gdn_nw=p["gdn_norm_w"][l][None, :],
        wg=p["w_gate"][l].astype(BF16), wu=p["w_up"][l].astype(BF16), wd=p["w_down"][l].astype(BF16),
    )


def _trunk(x, mod, ssd_h0, ssd_buf0, gdn_s0, gdn_buf0, p, layers):
    bsz, seq_len, d = x.shape
    n = bsz * seq_len
    x2 = x.reshape(n, d)
    n_rows = -(-(2 * n) // MOE_BLOCK) * MOE_BLOCK + N_EXPERTS * MOE_BLOCK
    qg = min(GMLP_CHUNK, seq_len)
    wrt = p["w_router"].T.astype(BF16)
    br = p["b_router"].astype(F32)[:, None]
    fw = p["norm_final"][None, :]
    outs = ([], [], [], [], [])
    depth = len(layers)
    for l, lp in enumerate(layers):
        sh1, sc1, ga1, sh2, sc2, ga2 = (m[:, None, :] for m in jnp.split(mod[l], 6, axis=-1))
        xbc, z, u, v, qkv, gg, sm = _inproj(x2, sh1, sc1, lp["norm_mix"], lp["w_r"], seq_len)
        ya, ssd_ht, ssd_buf = _ssd(xbc, z, sm, jnp.swapaxes(ssd_h0[l], -1, -2), ssd_buf0[l], lp["ssd_cw"], lp["ssd_cb"],
                                   lp["ssd_dtb"], lp["ssd_alog"], lp["ssd_dvec"], lp["ssd_nw"], bsz, seq_len)
        yb, vn = _gmlp(u, v, lp["lng"], lp["lnb"], lp["ws"][:, :qg, :qg], lp["bs"][:, :qg].T, bsz, seq_len)
        yc, gdn_s, gdn_buf = _gdn(qkv, gg, sm, gdn_s0[l], gdn_buf0[l], lp["gdn_cw"], lp["gdn_dtb"], lp["gdn_alog"],
                                  lp["gdn_nw"], bsz, seq_len)
        x2, h2, eidx, wts_t = _outproj_route(x2, ya, yb, yc, lp["wa"], lp["wb"], lp["wc"], ga1, lp["norm_ffn"],
                                             sc2, sh2, wrt, br, seq_len)
        dest, blk_e, nblk = _positions(eidx, n_rows // MOE_BLOCK)
        xs = _dispatch(dest, h2, n_rows)
        ye = _experts(blk_e[0], nblk[0, :1], xs, lp["wg"], lp["wu"], lp["wd"])
        x2 = _combine(dest, x2, wts_t, ga2, fw, ye, seq_len, final_norm=(l == depth - 1))
        for lst, s in zip(outs, (jnp.swapaxes(ssd_ht, -1, -2), ssd_buf, gdn_s, gdn_buf, vn.reshape(bsz, seq_len, GMLP_D))):
            lst.append(s)
    return (x2.reshape(bsz, seq_len, d),) + tuple(jnp.stack(o) for o in outs)


def kernel(x_prompt, x_sample, state_ssd, state_ssd_conv, state_gdn, state_gdn_conv, c_prompt, c_sample,
           w_ada, b_ada, norm_mix, norm_ffn, norm_final, w_in, w_out,
           ssd_conv_w, ssd_conv_b, ssd_dt_bias, ssd_a_log, ssd_d, ssd_norm_w,
           gmlp_ln_g, gmlp_ln_b, gmlp_ws, gmlp_bs,
           gdn_conv_w, gdn_a_log, gdn_dt_bias, gdn_norm_w,
           w_router, b_router, w_gate, w_up, w_down):
    p = dict(norm_mix=norm_mix, norm_ffn=norm_ffn, norm_final=norm_final, w_in=w_in, w_out=w_out,
             ssd_conv_w=ssd_conv_w, ssd_conv_b=ssd_conv_b, ssd_dt_bias=ssd_dt_bias, ssd_a_log=ssd_a_log,
             ssd_d=ssd_d, ssd_norm_w=ssd_norm_w, gmlp_ln_g=gmlp_ln_g, gmlp_ln_b=gmlp_ln_b, gmlp_ws=gmlp_ws,
             gmlp_bs=gmlp_bs, gdn_conv_w=gdn_conv_w, gdn_a_log=gdn_a_log, gdn_dt_bias=gdn_dt_bias,
             gdn_norm_w=gdn_norm_w, w_router=w_router, b_router=b_router, w_gate=w_gate, w_up=w_up, w_down=w_down)
    depth = w_in.shape[0]
    layers = [_layer_params(p, l) for l in range(depth)]
    bp = x_prompt.shape[0]
    mod = _ada_mod(jnp.concatenate([c_prompt, c_sample], axis=0), w_ada.astype(BF16), b_ada)
    z_ssd = jnp.zeros((depth, bp) + state_ssd.shape[2:], state_ssd.dtype)
    z_ssd_conv = jnp.zeros((depth, bp) + state_ssd_conv.shape[2:], x_prompt.dtype)
    z_gdn = jnp.zeros((depth, bp) + state_gdn.shape[2:], state_gdn.dtype)
    z_gdn_conv = jnp.zeros((depth, bp) + state_gdn_conv.shape[2:], x_prompt.dtype)
    y_p, p_ssd, p_ssd_conv, p_gdn, p_gdn_conv, _ = _trunk(
        x_prompt, mod[:, :bp], z_ssd, z_ssd_conv, z_gdn, z_gdn_conv, p, layers)
    y_s, s_ssd, s_ssd_conv, s_gdn, s_gdn_conv, s_gmlp_v = _trunk(
        x_sample, mod[:, bp:], state_ssd, state_ssd_conv, state_gdn, state_gdn_conv, p, layers)
    return (y_p, y_s, p_ssd, p_ssd_conv, p_gdn, p_gdn_conv, s_ssd, s_ssd_conv, s_gdn, s_gdn_conv, s_gmlp_v)
```

```python
import functools

import jax
import jax.numpy as jnp
from jax import lax
from jax.experimental import pallas as pl
from jax.experimental.pallas import tpu as pltpu

F32 = jnp.float32
BF16 = jnp.bfloat16
I32 = jnp.int32

LANES = 128
SUBLANES = 8
VMEM_LIMIT_BYTES = 56 * 1024 * 1024

D_MODEL = 1024
EPS = 1e-6
CHUNK = 64
CONV_W = 4
SSD_HEADS = 6
SSD_HEAD_DIM = 64
SSD_STATE = 64
SSD_GROUPS = 2
SSD_D = SSD_HEADS * SSD_HEAD_DIM
SSD_CONV_DIM = SSD_D + 2 * SSD_GROUPS * SSD_STATE
GMLP_GROUPS = 4
GMLP_GROUP_DIM = 64
GMLP_D = GMLP_GROUPS * GMLP_GROUP_DIM
GMLP_CHUNK = 128
GDN_HEADS = 6
GDN_KEY_DIM = 64
GDN_VAL_DIM = 64
GDN_QK_D = GDN_HEADS * GDN_KEY_DIM
GDN_V_D = GDN_HEADS * GDN_VAL_DIM
GDN_CONV_DIM = 2 * GDN_QK_D + GDN_V_D
N_EXPERTS = 16
N_EXPERT_GROUPS = 4
EXPERTS_PER_GROUP = N_EXPERTS // N_EXPERT_GROUPS
D_EXPERT = 512

SEG_WIDTHS = (SSD_CONV_DIM, SSD_D, GMLP_D, GMLP_D, GDN_CONV_DIM, GDN_V_D, LANES)
IN_R = sum(SEG_WIDTHS)
SSD_DT_LANE = 0
GDN_A_LANE = 8
GDN_B_LANE = 16
TAIL_ROW = SUBLANES - (CONV_W - 1)

MOE_BLOCK = 256
ROW_TILE = 256
POS_TILE = 2048
DMA_TILE = 256
DMA_UNROLL = 8


def _cparams(*sem):
    return pltpu.CompilerParams(dimension_semantics=sem, vmem_limit_bytes=VMEM_LIMIT_BYTES)


def _bdot(a, b):
    return jnp.dot(a.astype(BF16), b.astype(BF16), preferred_element_type=F32)


def _bdot_nt(a, b):
    return lax.dot_general(a.astype(BF16), b.astype(BF16), (((1,), (1,)), ((), ())), preferred_element_type=F32)


def _bdot_tn(a, b):
    return lax.dot_general(a.astype(BF16), b.astype(BF16), (((0,), (0,)), ((), ())), preferred_element_type=F32)


def _sigmoid(x):
    return 1.0 / (1.0 + jnp.exp(-x))


def _silu(x):
    return x * _sigmoid(x)


def _softplus(x):
    return jnp.maximum(x, 0.0) + jnp.log1p(jnp.exp(-jnp.abs(x)))


def _gelu_tanh(x):
    return 0.5 * x * (1.0 + jnp.tanh(0.7978845608028654 * (x + 0.044715 * (x * x * x))))


def _cumsum_rows(a):
    n = a.shape[0]
    row = lax.broadcasted_iota(I32, a.shape, 0)
    s = 1
    while s < n:
        a = a + jnp.where(row >= s, pltpu.roll(a, s, axis=0), jnp.zeros_like(a))
        s *= 2
    return a


def _cumsum_lanes(a):
    n = a.shape[1]
    col = lax.broadcasted_iota(I32, a.shape, 1)
    s = 1
    while s < n:
        a = a + jnp.where(col >= s, pltpu.roll(a, s, axis=1), jnp.zeros_like(a))
        s *= 2
    return a


def _rows_to_lanes(a):
    q = a.shape[0]
    if q < LANES:
        a = jnp.concatenate([a, jnp.zeros((LANES - q, LANES), a.dtype)], axis=0)
    return a.T[:, :q]


def _ada_body(c_ref, w_ref, b_ref, o_ref):
    o_ref[0] = _bdot(_silu(c_ref[...]), w_ref[0]) + b_ref[0]


def _ada_mod(c_all, w_ada, b_ada):
    depth, d, n6 = w_ada.shape
    bc = c_all.shape[0]
    tn = 1536
    return pl.pallas_call(
        _ada_body,
        grid=(depth, n6 // tn),
        in_specs=[pl.BlockSpec((bc, d), lambda l, j: (0, 0)),
                  pl.BlockSpec((1, d, tn), lambda l, j: (l, 0, j)),
                  pl.BlockSpec((1, 1, tn), lambda l, j: (l, 0, j))],
        out_specs=pl.BlockSpec((1, bc, tn), lambda l, j: (l, 0, j)),
        out_shape=jax.ShapeDtypeStruct((depth, bc, n6), F32),
        compiler_params=_cparams("parallel", "parallel"),
        name="ada_mod",
    )(c_all, w_ada, b_ada.reshape(depth, 1, n6))


def _seq_block(tm, seq_len, d):
    if tm <= seq_len:
        per_seq = seq_len // tm
        return pl.BlockSpec((1, 1, d), lambda i: (i // per_seq, 0, 0))
    return pl.BlockSpec((tm // seq_len, 1, d), lambda i: (i, 0, 0))


def _seq_rows(ref, tm):
    s = ref.shape[0]
    if s == 1:
        return ref[0]
    return jnp.concatenate([jnp.broadcast_to(ref[k], (tm // s, ref.shape[2])) for k in range(s)], axis=0)


def _mod_rms(x, nw, sc, sh):
    ms = jnp.mean(x * x, axis=-1, keepdims=True)
    return (x * lax.rsqrt(ms + EPS) * nw) * (1.0 + sc) + sh


def _inproj_body(x_ref, sh_ref, sc_ref, nw_ref, w_ref, *out_refs):
    tm = x_ref.shape[0]
    hb = _mod_rms(x_ref[...], nw_ref[...], _seq_rows(sc_ref, tm), _seq_rows(sh_ref, tm)).astype(BF16)
    off = 0
    for ref, width in zip(out_refs, SEG_WIDTHS):
        ref[...] = jnp.dot(hb, w_ref[:, off:off + width], preferred_element_type=F32)
        off += width


def _inproj(x2, sh, sc, nw, w_r, seq_len):
    n, d = x2.shape
    tm = min(ROW_TILE, n)
    row = lambda i: (i, 0)
    return pl.pallas_call(
        _inproj_body,
        grid=(n // tm,),
        in_specs=[pl.BlockSpec((tm, d), row),
                  _seq_block(tm, seq_len, d),
                  _seq_block(tm, seq_len, d),
                  pl.BlockSpec((1, d), lambda i: (0, 0)),
                  pl.BlockSpec((d, IN_R), lambda i: (0, 0))],
        out_specs=[pl.BlockSpec((tm, w), row) for w in SEG_WIDTHS],
        out_shape=[jax.ShapeDtypeStruct((n, w), F32) for w in SEG_WIDTHS],
        compiler_params=_cparams("parallel"),
        name="inproj",
    )(x2, sh, sc, nw, w_r)


def _causal_conv_step(x_ref, buf0_ref, bufout_ref, xp_ref, cw_ref, first):
    q = x_ref.shape[0]

    @pl.when(first)
    def _():
        xp_ref[TAIL_ROW:SUBLANES, :] = buf0_ref[0]

    xp_ref[SUBLANES:SUBLANES + q, :] = x_ref[...]
    acc = cw_ref[0:1, :] * xp_ref[TAIL_ROW:TAIL_ROW + q, :]
    for k in range(1, CONV_W):
        acc = acc + cw_ref[k:k + 1, :] * xp_ref[TAIL_ROW + k:TAIL_ROW + k + q, :]
    tail = xp_ref[TAIL_ROW + q:SUBLANES + q, :]
    bufout_ref[0] = tail
    xp_ref[TAIL_ROW:SUBLANES, :] = tail
    return acc


def _decay_terms(g):
    q = g.shape[0]
    cum = _cumsum_rows(g)
    cum_t = _rows_to_lanes(cum)
    last = cum[q - 1:q, :]
    return cum, cum_t, jnp.exp(cum), jnp.exp(last - cum), jnp.exp(last)


def _ssd_body(xbc_ref, z_ref, sm_ref, h0_ref, buf0_ref, cw_ref, cb_ref, dtb_ref, alog_ref, dvec_ref, nw_ref,
              y_ref, hout_ref, bufout_ref, xp_ref, ht_ref, ys_ref):
    q = xbc_ref.shape[0]
    first = pl.program_id(1) == 0

    @pl.when(first)
    def _():
        ht_ref[...] = h0_ref[0]

    xa = _silu(_causal_conv_step(xbc_ref, buf0_ref, bufout_ref, xp_ref, cw_ref, first) + cb_ref[...])
    xs = xa[:, :SSD_D]
    bm = xa[:, SSD_D:SSD_D + SSD_GROUPS * SSD_STATE]
    cm = xa[:, SSD_D + SSD_GROUPS * SSD_STATE:]
    dt = _softplus(sm_ref[...] + dtb_ref[...])
    cum, cum_t, ecum, wend, edec = _decay_terms(dt * (-jnp.exp(alog_ref[...])))
    causal = lax.broadcasted_iota(I32, (q, q), 0) >= lax.broadcasted_iota(I32, (q, q), 1)
    heads_per_group = SSD_HEADS // SSD_GROUPS
    for g in range(SSD_GROUPS):
        bg = bm[:, g * SSD_STATE:(g + 1) * SSD_STATE]
        cg = cm[:, g * SSD_STATE:(g + 1) * SSD_STATE]
        cb = _bdot_nt(cg, bg)
        for h in range(g * heads_per_group, (g + 1) * heads_per_group):
            lane = SSD_DT_LANE + h
            hs = slice(h * SSD_HEAD_DIM, (h + 1) * SSD_HEAD_DIM)
            seg = cum[:, lane:lane + 1] - cum_t[lane:lane + 1, :]
            scores = cb * jnp.where(causal, jnp.exp(seg), 0.0)
            xh = xs[:, hs]
            xdt = xh * dt[:, lane:lane + 1]
            ht = ht_ref[h]
            y = _bdot(scores, xdt) + _bdot(cg * ecum[:, lane:lane + 1], ht)
            ys_ref[:, hs] = y + dvec_ref[:, hs] * xh
            ht_ref[h] = ht * edec[:, lane:lane + 1] + _bdot_tn(bg * wend[:, lane:lane + 1], xdt)
    y = ys_ref[...] * _silu(z_ref[...])
    half = SSD_D // SSD_GROUPS
    in_g0 = lax.broadcasted_iota(I32, y.shape, 1) < half
    y2 = y * y
    ss0 = jnp.sum(jnp.where(in_g0, y2, 0.0), axis=-1, keepdims=True)
    ss1 = jnp.sum(jnp.where(in_g0, 0.0, y2), axis=-1, keepdims=True)
    scale = jnp.where(in_g0, lax.rsqrt(ss0 / half + EPS), lax.rsqrt(ss1 / half + EPS))
    y_ref[...] = y * scale * nw_ref[...]
    hout_ref[0] = ht_ref[...]


def _ssd(xbc, z, sm, h0t, buf0, cw, cb, dtb, alog, dvec, nw, bsz, seq_len):
    q = min(CHUNK, seq_len)
    nc = seq_len // q
    n = bsz * seq_len
    tok = lambda b, c: (b * nc + c, 0)
    per_b4 = lambda b, c: (b, 0, 0, 0)
    per_b3 = lambda b, c: (b, 0, 0)
    const = lambda b, c: (0, 0)
    return pl.pallas_call(
        _ssd_body,
        grid=(bsz, nc),
        in_specs=[pl.BlockSpec((q, SSD_CONV_DIM), tok),
                  pl.BlockSpec((q, SSD_D), tok),
                  pl.BlockSpec((q, LANES), tok),
                  pl.BlockSpec((1, SSD_HEADS, SSD_STATE, SSD_HEAD_DIM), per_b4),
                  pl.BlockSpec((1, CONV_W - 1, SSD_CONV_DIM), per_b3),
                  pl.BlockSpec((CONV_W, SSD_CONV_DIM), const),
                  pl.BlockSpec((1, SSD_CONV_DIM), const),
                  pl.BlockSpec((1, LANES), const),
                  pl.BlockSpec((1, LANES), const),
                  pl.BlockSpec((1, SSD_D), const),
                  pl.BlockSpec((1, SSD_D), const)],
        out_specs=[pl.BlockSpec((q, SSD_D), tok),
                   pl.BlockSpec((1, SSD_HEADS, SSD_STATE, SSD_HEAD_DIM), per_b4),
                   pl.BlockSpec((1, CONV_W - 1, SSD_CONV_DIM), per_b3)],
        out_shape=[jax.ShapeDtypeStruct((n, SSD_D), F32),
                   jax.ShapeDtypeStruct((bsz, SSD_HEADS, SSD_STATE, SSD_HEAD_DIM), F32),
                   jax.ShapeDtypeStruct((bsz, CONV_W - 1, SSD_CONV_DIM), F32)],
        scratch_shapes=[pltpu.VMEM((SUBLANES + q, SSD_CONV_DIM), F32),
                        pltpu.VMEM((SSD_HEADS, SSD_STATE, SSD_HEAD_DIM), F32),
                        pltpu.VMEM((q, SSD_D), F32)],
        compiler_params=_cparams("parallel", "arbitrary"),
        name="ssd",
    )(xbc, z, sm, h0t, buf0, cw, cb, dtb, alog, dvec, nw)


def _gmlp_body(u_ref, v_ref, lng_ref, lnb_ref, ws_ref, bst_ref, y_ref, vn_ref):
    q = u_ref.shape[0]
    gu = _gelu_tanh(u_ref[...])
    gv = _gelu_tanh(v_ref[...])
    lower = lax.broadcasted_iota(I32, (q, q), 0) >= lax.broadcasted_iota(I32, (q, q), 1)
    for g in range(GMLP_GROUPS):
        gs = slice(g * GMLP_GROUP_DIM, (g + 1) * GMLP_GROUP_DIM)
        vg = gv[:, gs]
        mu = jnp.mean(vg, axis=-1, keepdims=True)
        dv = vg - mu
        var = jnp.mean(dv * dv, axis=-1, keepdims=True)
        vn = dv * lax.rsqrt(var + EPS) * lng_ref[:, gs] + lnb_ref[:, gs]
        vn_ref[:, gs] = vn
        mixed = _bdot(jnp.where(lower, ws_ref[g], 0.0), vn) + bst_ref[:, g:g + 1]
        y_ref[:, gs] = gu[:, gs] * mixed


def _gmlp(u, v, lng, lnb, ws, bst, bsz, seq_len):
    q = min(GMLP_CHUNK, seq_len)
    n = bsz * seq_len
    tok = lambda i: (i, 0)
    return pl.pallas_call(
        _gmlp_body,
        grid=(n // q,),
        in_specs=[pl.BlockSpec((q, GMLP_D), tok),
                  pl.BlockSpec((q, GMLP_D), tok),
                  pl.BlockSpec((1, GMLP_D), lambda i: (0, 0)),
                  pl.BlockSpec((1, GMLP_D), lambda i: (0, 0)),
                  pl.BlockSpec((GMLP_GROUPS, q, q), lambda i: (0, 0, 0)),
                  pl.BlockSpec((q, GMLP_GROUPS), lambda i: (0, 0))],
        out_specs=[pl.BlockSpec((q, GMLP_D), tok), pl.BlockSpec((q, GMLP_D), tok)],
        out_shape=[jax.ShapeDtypeStruct((n, GMLP_D), F32), jax.ShapeDtypeStruct((n, GMLP_D), F32)],
        compiler_params=_cparams("parallel"),
        name="gmlp",
    )(u, v, lng, lnb, ws, bst)


def _l2_rows(x):
    return x * lax.rsqrt(jnp.sum(x * x, axis=-1, keepdims=True) + EPS)


def _unit_lower_inverse_minus_identity(mats):
    q = mats[0].shape[0]
    ii = lax.broadcasted_iota(I32, (q, q), 0)
    jj = lax.broadcasted_iota(I32, (q, q), 1)
    base = (ii >> 1) == (jj >> 1)
    rs = [-jnp.where(base, a, 0.0) for a in mats]
    level = 1
    while (1 << level) < q:
        in_pair = (ii >> (level + 1)) == (jj >> (level + 1))
        off_diag = (ii >> level) != (jj >> level)
        mask = jnp.logical_and(in_pair, off_diag)
        ams = [jnp.where(mask, a, 0.0) for a in mats]
        ws = [am + _bdot(am, r) for am, r in zip(ams, rs)]
        rs = [r - w - _bdot(r, w) for r, w in zip(rs, ws)]
        level += 1
    return rs


def _gdn_body(qkv_ref, gg_ref, sm_ref, s0_ref, buf0_ref, cw_ref, dtb_ref, alog_ref, nw_ref,
              y_ref, sout_ref, bufout_ref, xp_ref, s_ref):
    q = qkv_ref.shape[0]
    first = pl.program_id(1) == 0

    @pl.when(first)
    def _():
        s_ref[...] = s0_ref[0]

    xa = _silu(_causal_conv_step(qkv_ref, buf0_ref, bufout_ref, xp_ref, cw_ref, first))
    sm = sm_ref[...]
    beta_all = _sigmoid(sm)
    g_all = -jnp.exp(alog_ref[...]) * _softplus(sm + dtb_ref[...])
    cum, cum_t, ecum, kdecw, edec = _decay_terms(g_all)
    ii = lax.broadcasted_iota(I32, (q, q), 0)
    jj = lax.broadcasted_iota(I32, (q, q), 1)
    gg = gg_ref[...]
    heads = range(GDN_HEADS)
    col = lambda m, h: m[:, GDN_A_LANE + h:GDN_A_LANE + h + 1]
    s_old = [s_ref[h] for h in heads]
    qh = [_l2_rows(xa[:, h * GDN_KEY_DIM:(h + 1) * GDN_KEY_DIM]) * (GDN_KEY_DIM ** -0.5) for h in heads]
    kh = [_l2_rows(xa[:, GDN_QK_D + h * GDN_KEY_DIM:GDN_QK_D + (h + 1) * GDN_KEY_DIM]) for h in heads]
    vh = [xa[:, 2 * GDN_QK_D + h * GDN_VAL_DIM:2 * GDN_QK_D + (h + 1) * GDN_VAL_DIM] for h in heads]
    beta = [beta_all[:, GDN_B_LANE + h:GDN_B_LANE + h + 1] for h in heads]
    kb = [kh[h] * beta[h] for h in heads]
    e = [jnp.exp(col(cum, h) - cum_t[GDN_A_LANE + h:GDN_A_LANE + h + 1, :]) for h in heads]
    kk = [_bdot_nt(jnp.concatenate([kb[h], qh[h]], axis=0), kh[h]) for h in heads]
    a_mat = [kk[h][:q] * jnp.where(ii > jj, e[h], 0.0) for h in heads]
    attn = [kk[h][q:] * jnp.where(ii >= jj, e[h], 0.0) for h in heads]
    r = _unit_lower_inverse_minus_identity(a_mat)
    vb = [vh[h] * beta[h] for h in heads]
    kbe = [kb[h] * col(ecum, h) for h in heads]
    u_base = [vb[h] + _bdot(r[h], vb[h]) for h in heads]
    k_cd = [kbe[h] + _bdot(r[h], kbe[h]) for h in heads]
    ks = [_bdot(jnp.concatenate([k_cd[h], qh[h] * col(ecum, h)], axis=0), s_old[h]) for h in heads]
    u = [u_base[h] - ks[h][:q] for h in heads]
    au = [_bdot(attn[h], u[h]) for h in heads]
    ku = [_bdot_tn(kh[h] * col(kdecw, h), u[h]) for h in heads]
    outs = []
    for h in heads:
        s_new = s_old[h] * col(edec, h) + ku[h]
        s_ref[h] = s_new
        sout_ref[0, h] = s_new
        o = ks[h][q:] + au[h]
        o = o * lax.rsqrt(jnp.mean(o * o, axis=-1, keepdims=True) + EPS) * nw_ref[...]
        outs.append(o * _silu(gg[:, h * GDN_VAL_DIM:(h + 1) * GDN_VAL_DIM]))
    y_ref[...] = jnp.concatenate(outs, axis=1)


def _gdn(qkv, gg, sm, s0, buf0, cw, dtb, alog, nw, bsz, seq_len):
    q = min(CHUNK, seq_len)
    nc = seq_len // q
    n = bsz * seq_len
    tok = lambda b, c: (b * nc + c, 0)
    per_b4 = lambda b, c: (b, 0, 0, 0)
    per_b3 = lambda b, c: (b, 0, 0)
    const = lambda b, c: (0, 0)
    return pl.pallas_call(
        _gdn_body,
        grid=(bsz, nc),
        in_specs=[pl.BlockSpec((q, GDN_CONV_DIM), tok),
                  pl.BlockSpec((q, GDN_V_D), tok),
                  pl.BlockSpec((q, LANES), tok),
                  pl.BlockSpec((1, GDN_HEADS, GDN_KEY_DIM, GDN_VAL_DIM), per_b4),
                  pl.BlockSpec((1, CONV_W - 1, GDN_CONV_DIM), per_b3),
                  pl.BlockSpec((CONV_W, GDN_CONV_DIM), const),
                  pl.BlockSpec((1, LANES), const),
                  pl.BlockSpec((1, LANES), const),
                  pl.BlockSpec((1, GDN_VAL_DIM), const)],
        out_specs=[pl.BlockSpec((q, GDN_V_D), tok),
                   pl.BlockSpec((1, GDN_HEADS, GDN_KEY_DIM, GDN_VAL_DIM), per_b4),
                   pl.BlockSpec((1, CONV_W - 1, GDN_CONV_DIM), per_b3)],
        out_shape=[jax.ShapeDtypeStruct((n, GDN_V_D), F32),
                   jax.ShapeDtypeStruct((bsz, GDN_HEADS, GDN_KEY_DIM, GDN_VAL_DIM), F32),
                   jax.ShapeDtypeStruct((bsz, CONV_W - 1, GDN_CONV_DIM), F32)],
        scratch_shapes=[pltpu.VMEM((SUBLANES + q, GDN_CONV_DIM), F32),
                        pltpu.VMEM((GDN_HEADS, GDN_KEY_DIM, GDN_VAL_DIM), F32)],
        compiler_params=_cparams("parallel", "arbitrary"),
        name="gdn",
    )(qkv, gg, sm, s0, buf0, cw, dtb, alog, nw)


def _route(logits, b_col):
    aff = _sigmoid(logits)
    sel = aff + b_col
    srow = [sel[e:e + 1, :] for e in range(N_EXPERTS)]
    arow = [aff[e:e + 1, :] for e in range(N_EXPERTS)]
    gscore = []
    for g in range(N_EXPERT_GROUPS):
        a, b, c, d = srow[g * EXPERTS_PER_GROUP:(g + 1) * EXPERTS_PER_GROUP]
        hi1, lo1, hi2, lo2 = jnp.maximum(a, b), jnp.minimum(a, b), jnp.maximum(c, d), jnp.minimum(c, d)
        gscore.append(jnp.maximum(hi1, hi2) + jnp.maximum(jnp.minimum(hi1, hi2), jnp.maximum(lo1, lo2)))
    best = jnp.zeros(gscore[0].shape, I32)
    best_v = gscore[0]
    for g in range(1, N_EXPERT_GROUPS):
        upd = gscore[g] > best_v
        best = jnp.where(upd, g, best)
        best_v = jnp.where(upd, gscore[g], best_v)
    cs, ca = [], []
    for j in range(EXPERTS_PER_GROUP):
        cj, aj = srow[j], arow[j]
        for g in range(1, N_EXPERT_GROUPS):
            m = best == g
            cj = jnp.where(m, srow[g * EXPERTS_PER_GROUP + j], cj)
            aj = jnp.where(m, arow[g * EXPERTS_PER_GROUP + j], aj)
        cs.append(cj)
        ca.append(aj)
    i1 = jnp.zeros(best.shape, I32)
    v1, a1 = cs[0], ca[0]
    for j in range(1, EXPERTS_PER_GROUP):
        upd = cs[j] > v1
        i1 = jnp.where(upd, j, i1)
        v1 = jnp.where(upd, cs[j], v1)
        a1 = jnp.where(upd, ca[j], a1)
    i2 = jnp.zeros(best.shape, I32)
    v2 = jnp.full(v1.shape, -jnp.inf, F32)
    a2 = jnp.zeros(v1.shape, F32)
    for j in range(EXPERTS_PER_GROUP):
        upd = jnp.logical_and(i1 != j, cs[j] > v2)
        i2 = jnp.where(upd, j, i2)
        v2 = jnp.where(upd, cs[j], v2)
        a2 = jnp.where(upd, ca[j], a2)
    tot = a1 + a2
    return best * EXPERTS_PER_GROUP + i1, best * EXPERTS_PER_GROUP + i2, a1 / tot, a2 / tot


def _outproj_body(x_ref, ya_ref, yb_ref, yc_ref, wa_ref, wb_ref, wc_ref, ga_ref, nw_ref, sc_ref, sh_ref,
                  wrt_ref, br_ref, xo_ref, h2_ref, eidx_ref, wt_ref):
    m = _bdot(ya_ref[...], wa_ref[...]) + _bdot(yb_ref[...], wb_ref[...]) + _bdot(yc_ref[...], wc_ref[...])
    tm = x_ref.shape[0]
    x = x_ref[...] + _seq_rows(ga_ref, tm) * m
    xo_ref[...] = x
    h2 = _mod_rms(x, nw_ref[...], _seq_rows(sc_ref, tm), _seq_rows(sh_ref, tm))
    h2_ref[...] = h2
    e1, e2, w1, w2 = _route(_bdot_nt(wrt_ref[...], h2), br_ref[...])
    eidx_ref[0:1, :] = e1
    eidx_ref[1:2, :] = e2
    row = lax.broadcasted_iota(I32, (LANES, w1.shape[1]), 0)
    wt_ref[...] = jnp.where(row == 0, w1, jnp.where(row == 1, w2, 0.0)).T


def _outproj_route(x2, ya, yb, yc, wa, wb, wc, ga1, nw, sc2, sh2, wrt, br, seq_len):
    n, d = x2.shape
    tm = min(ROW_TILE, n)
    row = lambda i: (i, 0)
    seq = _seq_block(tm, seq_len, d)
    const = lambda i: (0, 0)
    return pl.pallas_call(
        _outproj_body,
        grid=(n // tm,),
        in_specs=[pl.BlockSpec((tm, d), row),
                  pl.BlockSpec((tm, SSD_D), row),
                  pl.BlockSpec((tm, GMLP_D), row),
                  pl.BlockSpec((tm, GDN_V_D), row),
                  pl.BlockSpec((SSD_D, d), const),
                  pl.BlockSpec((GMLP_D, d), const),
                  pl.BlockSpec((GDN_V_D, d), const),
                  seq,
                  pl.BlockSpec((1, d), const),
                  seq,
                  seq,
                  pl.BlockSpec((N_EXPERTS, d), const),
                  pl.BlockSpec((N_EXPERTS, 1), const)],
        out_specs=[pl.BlockSpec((tm, d), row),
                   pl.BlockSpec((tm, d), row),
                   pl.BlockSpec((2, tm), lambda i: (0, i)),
                   pl.BlockSpec((tm, LANES), row)],
        out_shape=[jax.ShapeDtypeStruct((n, d), F32),
                   jax.ShapeDtypeStruct((n, d), F32),
                   jax.ShapeDtypeStruct((2, n), I32),
                   jax.ShapeDtypeStruct((n, LANES), F32)],
        compiler_params=_cparams("parallel"),
        name="outproj_route",
    )(x2, ya, yb, yc, wa, wb, wc, ga1, nw, sc2, sh2, wrt, br)


def _positions_body(eidx_ref, dest_ref, blk_ref, nblk_ref, cnt_ref, run_ref):
    phase = pl.program_id(0)
    i = pl.program_id(1)
    t = eidx_ref.shape[1]
    e_iota = lax.broadcasted_iota(I32, (N_EXPERTS, t), 0)
    m0 = e_iota == eidx_ref[0:1, :]
    m1 = e_iota == eidx_ref[1:2, :]
    hits = jnp.where(jnp.logical_or(m0, m1), 1.0, 0.0)
    tile_cnt = jnp.broadcast_to(jnp.sum(hits, axis=1, keepdims=True), (N_EXPERTS, LANES))

    @pl.when(jnp.logical_and(phase == 0, i == 0))
    def _():
        cnt_ref[...] = jnp.zeros_like(cnt_ref)

    @pl.when(phase == 0)
    def _():
        cnt_ref[...] += tile_cnt

    @pl.when(jnp.logical_and(phase == 1, i == 0))
    def _():
        padded = jnp.floor((cnt_ref[...] + (MOE_BLOCK - 1)) * (1.0 / MOE_BLOCK)) * MOE_BLOCK
        pad_end = _cumsum_rows(padded)
        run_ref[...] = pad_end - padded
        nb = blk_ref.shape[1]
        blk_start = (lax.broadcasted_iota(I32, (N_EXPERTS, nb), 1) * MOE_BLOCK).astype(F32)
        past = jnp.where(blk_start >= pad_end[:, 0:1], 1.0, 0.0)
        blk_ref[...] = jnp.minimum(jnp.sum(past, axis=0, keepdims=True), N_EXPERTS - 1.0).astype(I32)
        nblk_ref[...] = (pad_end[N_EXPERTS - 1:N_EXPERTS, :] * (1.0 / MOE_BLOCK)).astype(I32)

    @pl.when(phase == 1)
    def _():
        excl = _cumsum_lanes(hits) - hits
        pos = run_ref[:, 0:1] + excl
        dest_ref[0:1, :] = jnp.sum(jnp.where(m0, pos, 0.0), axis=0, keepdims=True).astype(I32)
        dest_ref[1:2, :] = jnp.sum(jnp.where(m1, pos, 0.0), axis=0, keepdims=True).astype(I32)
        run_ref[...] += tile_cnt


def _positions(eidx, n_blocks):
    n = eidx.shape[1]
    t = min(POS_TILE, n)
    nb_pad = -(-n_blocks // LANES) * LANES
    return pl.pallas_call(
        _positions_body,
        grid=(2, n // t),
        in_specs=[pl.BlockSpec((2, t), lambda p, i: (0, i))],
        out_specs=[pl.BlockSpec((2, t), lambda p, i: (0, i * p)),
                   pl.BlockSpec((1, nb_pad), lambda p, i: (0, 0)),
                   pl.BlockSpec((1, LANES), lambda p, i: (0, 0))],
        out_shape=[jax.ShapeDtypeStruct((2, n), I32),
                   jax.ShapeDtypeStruct((1, nb_pad), I32),
                   jax.ShapeDtypeStruct((1, LANES), I32)],
        scratch_shapes=[pltpu.VMEM((N_EXPERTS, LANES), F32), pltpu.VMEM((N_EXPERTS, LANES), F32)],
        compiler_params=_cparams("arbitrary", "arbitrary"),
        name="moe_positions",
    )(eidx)


def _row_copy(src_ref, src_row, dst_ref, dst_row, sem):
    return pltpu.make_async_copy(src_ref.at[pl.ds(src_row, 1), :], dst_ref.at[pl.ds(dst_row, 1), :], sem)


def _dispatch_body(dest_ref, h_ref, xs_in_ref, xs_ref, sem):
    del xs_in_ref
    t = dest_ref.shape[1]

    def issue(jo, carry):
        for k in range(DMA_UNROLL):
            j = jo * DMA_UNROLL + k
            _row_copy(h_ref, j, xs_ref, dest_ref[0, j], sem.at[0]).start()
            _row_copy(h_ref, j, xs_ref, dest_ref[1, j], sem.at[1]).start()
        return carry

    lax.fori_loop(0, t // DMA_UNROLL, issue, 0)
    for k in range(2):
        pltpu.make_async_copy(h_ref, xs_ref.at[pl.ds(0, t), :], sem.at[k]).wait()


def _dispatch(dest, h2, n_rows):
    n, d = h2.shape
    t = min(DMA_TILE, n)
    xs0 = jnp.zeros((n_rows, d), F32)
    return pl.pallas_call(
        _dispatch_body,
        grid=(n // t,),
        in_specs=[pl.BlockSpec((2, t), lambda i: (0, i), memory_space=pltpu.SMEM),
                  pl.BlockSpec((t, d), lambda i: (i, 0)),
                  pl.BlockSpec(memory_space=pl.ANY)],
        out_specs=pl.BlockSpec(memory_space=pl.ANY),
        out_shape=jax.ShapeDtypeStruct((n_rows, d), F32),
        scratch_shapes=[pltpu.SemaphoreType.DMA((2,))],
        input_output_aliases={2: 0},
        compiler_params=_cparams("arbitrary"),
        name="moe_dispatch",
    )(dest, h2, xs0)


def _experts_body(blk_ref, nblk_ref, xs_ref, wg_ref, wu_ref, wd_ref, y_ref):
    used = pl.program_id(0) < nblk_ref[0]

    @pl.when(used)
    def _():
        x = xs_ref[...].astype(BF16)
        hid = _silu(jnp.dot(x, wg_ref[0], preferred_element_type=F32)) * jnp.dot(x, wu_ref[0], preferred_element_type=F32)
        y_ref[...] = jnp.dot(hid.astype(BF16), wd_ref[0], preferred_element_type=F32)

    @pl.when(jnp.logical_not(used))
    def _():
        y_ref[...] = jnp.zeros_like(y_ref)


def _experts(blk_e, nblk, xs, wg, wu, wd):
    n_rows, d = xs.shape
    nb = n_rows // MOE_BLOCK

    def live(b, blk, nbl):
        return jnp.minimum(b, nbl[0] - 1)

    rows = lambda b, blk, nbl: (live(b, blk, nbl), 0)
    wsel = lambda b, blk, nbl: (blk[live(b, blk, nbl)], 0, 0)
    return pl.pallas_call(
        _experts_body,
        grid_spec=pltpu.PrefetchScalarGridSpec(
            num_scalar_prefetch=2,
            grid=(nb,),
            in_specs=[pl.BlockSpec((MOE_BLOCK, d), rows),
                      pl.BlockSpec((1, d, D_EXPERT), wsel),
                      pl.BlockSpec((1, d, D_EXPERT), wsel),
                      pl.BlockSpec((1, D_EXPERT, d), wsel)],
            out_specs=pl.BlockSpec((MOE_BLOCK, d), lambda b, blk, nbl: (b, 0))),
        out_shape=jax.ShapeDtypeStruct((n_rows, d), F32),
        compiler_params=_cparams("arbitrary"),
        name="moe_experts",
    )(blk_e, nblk, xs, wg, wu, wd)


def _combine_body(dest_ref, x_ref, wt_ref, ga_ref, fw_ref, yb_ref, o_ref, r0_ref, r1_ref, sem, *, final_norm):
    t = dest_ref.shape[1]

    def issue(jo, carry):
        for k in range(DMA_UNROLL):
            j = jo * DMA_UNROLL + k
            _row_copy(yb_ref, dest_ref[0, j], r0_ref, j, sem.at[0]).start()
            _row_copy(yb_ref, dest_ref[1, j], r1_ref, j, sem.at[1]).start()
        return carry

    lax.fori_loop(0, t // DMA_UNROLL, issue, 0)
    pltpu.make_async_copy(yb_ref.at[pl.ds(0, t), :], r0_ref, sem.at[0]).wait()
    pltpu.make_async_copy(yb_ref.at[pl.ds(0, t), :], r1_ref, sem.at[1]).wait()
    wt = wt_ref[...]
    moe = r0_ref[...] * wt[:, 0:1] + r1_ref[...] * wt[:, 1:2]
    x = x_ref[...] + _seq_rows(ga_ref, t) * moe
    if final_norm:
        x = x * lax.rsqrt(jnp.mean(x * x, axis=-1, keepdims=True) + EPS) * fw_ref[...]
    o_ref[...] = x


def _combine(dest, x2, wts_t, ga2, fw, yb, seq_len, final_norm):
    n, d = x2.shape
    t = min(DMA_TILE, n)
    row = lambda i: (i, 0)
    return pl.pallas_call(
        functools.partial(_combine_body, final_norm=final_norm),
        grid=(n // t,),
        in_specs=[pl.BlockSpec((2, t), lambda i: (0, i), memory_space=pltpu.SMEM),
                  pl.BlockSpec((t, d), row),
                  pl.BlockSpec((t, LANES), row),
                  _seq_block(t, seq_len, d),
                  pl.BlockSpec((1, d), lambda i: (0, 0)),
                  pl.BlockSpec(memory_space=pl.ANY)],
        out_specs=pl.BlockSpec((t, d), row),
        out_shape=jax.ShapeDtypeStruct((n, d), F32),
        scratch_shapes=[pltpu.VMEM((t, d), F32), pltpu.VMEM((t, d), F32), pltpu.SemaphoreType.DMA((2,))],
        compiler_params=_cparams("arbitrary"),
        name="moe_combine",
    )(dest, x2, wts_t, ga2, fw, yb)


def _lane_vec(vals, lane):
    return jnp.zeros((1, LANES), F32).at[0, lane:lane + vals.shape[0]].set(vals.astype(F32))


def _layer_params(p, l):
    w_in = p["w_in"][l]
    cuts = [0]
    for s in (SSD_CONV_DIM, SSD_D, SSD_HEADS, GMLP_D, GMLP_D, GDN_CONV_DIM, GDN_HEADS, GDN_HEADS, GDN_V_D):
        cuts.append(cuts[-1] + s)
    xbc, z, dt, u, v, qkv, ga, gb, gg = (w_in[:, a:b] for a, b in zip(cuts[:-1], cuts[1:]))
    small = jnp.zeros((D_MODEL, LANES), F32)
    small = small.at[:, SSD_DT_LANE:SSD_DT_LANE + SSD_HEADS].set(dt)
    small = small.at[:, GDN_A_LANE:GDN_A_LANE + GDN_HEADS].set(ga)
    small = small.at[:, GDN_B_LANE:GDN_B_LANE + GDN_HEADS].set(gb)
    w_out = p["w_out"][l].astype(BF16)
    return dict(
        w_r=jnp.concatenate([xbc, z, u, v, qkv, gg, small], axis=1).astype(BF16),
        wa=w_out[:SSD_D], wb=w_out[SSD_D:SSD_D + GMLP_D], wc=w_out[SSD_D + GMLP_D:],
        norm_mix=p["norm_mix"][l][None, :], norm_ffn=p["norm_ffn"][l][None, :],
        ssd_cw=p["ssd_conv_w"][l], ssd_cb=p["ssd_conv_b"][l][None, :],
        ssd_dtb=_lane_vec(p["ssd_dt_bias"][l], SSD_DT_LANE), ssd_alog=_lane_vec(p["ssd_a_log"][l], SSD_DT_LANE),
        ssd_dvec=jnp.repeat(p["ssd_d"][l].astype(F32), SSD_HEAD_DIM)[None, :], ssd_nw=p["ssd_norm_w"][l][None, :],
        lng=p["gmlp_ln_g"][l][None, :], lnb=p["gmlp_ln_b"][l][None, :], ws=p["gmlp_ws"][l], bs=p["gmlp_bs"][l],
        gdn_cw=p["gdn_conv_w"][l],
        gdn_dtb=_lane_vec(p["gdn_dt_bias"][l], GDN_A_LANE), gdn_alog=_lane_vec(p["gdn_a_log"][l], GDN_A_LANE),
        gdn_nw=p["gdn_norm_w"][l][None, :],
        wg=p["w_gate"][l].astype(BF16), wu=p["w_up"][l].astype(BF16), wd=p["w_down"][l].astype(BF16),
    )


def _trunk(x, mod, ssd_h0, ssd_buf0, gdn_s0, gdn_buf0, p, layers):
    bsz, seq_len, d = x.shape
    n = bsz * seq_len
    x2 = x.reshape(n, d)
    n_rows = -(-(2 * n) // MOE_BLOCK) * MOE_BLOCK + N_EXPERTS * MOE_BLOCK
    qg = min(GMLP_CHUNK, seq_len)
    wrt = p["w_router"].T.astype(BF16)
    br = p["b_router"].astype(F32)[:, None]
    fw = p["norm_final"][None, :]
    outs = ([], [], [], [], [])
    depth = len(layers)
    for l, lp in enumerate(layers):
        sh1, sc1, ga1, sh2, sc2, ga2 = (m[:, None, :] for m in jnp.split(mod[l], 6, axis=-1))
        xbc, z, u, v, qkv, gg, sm = _inproj(x2, sh1, sc1, lp["norm_mix"], lp["w_r"], seq_len)
        ya, ssd_ht, ssd_buf = _ssd(xbc, z, sm, jnp.swapaxes(ssd_h0[l], -1, -2), ssd_buf0[l], lp["ssd_cw"], lp["ssd_cb"],
                                   lp["ssd_dtb"], lp["ssd_alog"], lp["ssd_dvec"], lp["ssd_nw"], bsz, seq_len)
        yb, vn = _gmlp(u, v, lp["lng"], lp["lnb"], lp["ws"][:, :qg, :qg], lp["bs"][:, :qg].T, bsz, seq_len)
        yc, gdn_s, gdn_buf = _gdn(qkv, gg, sm, gdn_s0[l], gdn_buf0[l], lp["gdn_cw"], lp["gdn_dtb"], lp["gdn_alog"],
                                  lp["gdn_nw"], bsz, seq_len)
        x2, h2, eidx, wts_t = _outproj_route(x2, ya, yb, yc, lp["wa"], lp["wb"], lp["wc"], ga1, lp["norm_ffn"],
                                             sc2, sh2, wrt, br, seq_len)
        dest, blk_e, nblk = _positions(eidx, n_rows // MOE_BLOCK)
        xs = _dispatch(dest, h2, n_rows)
        ye = _experts(blk_e[0], nblk[0, :1], xs, lp["wg"], lp["wu"], lp["wd"])
        x2 = _combine(dest, x2, wts_t, ga2, fw, ye, seq_len, final_norm=(l == depth - 1))
        for lst, s in zip(outs, (jnp.swapaxes(ssd_ht, -1, -2), ssd_buf, gdn_s, gdn_buf, vn.reshape(bsz, seq_len, GMLP_D))):
            lst.append(s)
    return (x2.reshape(bsz, seq_len, d),) + tuple(jnp.stack(o) for o in outs)


def kernel(x_prompt, x_sample, state_ssd, state_ssd_conv, state_gdn, state_gdn_conv, c_prompt, c_sample,
           w_ada, b_ada, norm_mix, norm_ffn, norm_final, w_in, w_out,
           ssd_conv_w, ssd_conv_b, ssd_dt_bias, ssd_a_log, ssd_d, ssd_norm_w,
           gmlp_ln_g, gmlp_ln_b, gmlp_ws, gmlp_bs,
           gdn_conv_w, gdn_a_log, gdn_dt_bias, gdn_norm_w,
           w_router, b_router, w_gate, w_up, w_down):
    p = dict(norm_mix=norm_mix, norm_ffn=norm_ffn, norm_final=norm_final, w_in=w_in, w_out=w_out,
             ssd_conv_w=ssd_conv_w, ssd_conv_b=ssd_conv_b, ssd_dt_bias=ssd_dt_bias, ssd_a_log=ssd_a_log,
             ssd_d=ssd_d, ssd_norm_w=ssd_norm_w, gmlp_ln_g=gmlp_ln_g, gmlp_ln_b=gmlp_ln_b, gmlp_ws=gmlp_ws,
             gmlp_bs=gmlp_bs, gdn_conv_w=gdn_conv_w, gdn_a_log=gdn_a_log, gdn_dt_bias=gdn_dt_bias,
             gdn_norm_w=gdn_norm_w, w_router=w_router, b_router=b_router, w_gate=w_gate, w_up=w_up, w_down=w_down)
    depth = w_in.shape[0]
    layers = [_layer_params(p, l) for l in range(depth)]
    bp = x_prompt.shape[0]
    mod = _ada_mod(jnp.concatenate([c_prompt, c_sample], axis=0), w_ada.astype(BF16), b_ada)
    z_ssd = jnp.zeros((depth, bp) + state_ssd.shape[2:], state_ssd.dtype)
    z_ssd_conv = jnp.zeros((depth, bp) + state_ssd_conv.shape[2:], x_prompt.dtype)
    z_gdn = jnp.zeros((depth, bp) + state_gdn.shape[2:], state_gdn.dtype)
    z_gdn_conv = jnp.zeros((depth, bp) + state_gdn_conv.shape[2:], x_prompt.dtype)
    y_p, p_ssd, p_ssd_conv, p_gdn, p_gdn_conv, _ = _trunk(
        x_prompt, mod[:, :bp], z_ssd, z_ssd_conv, z_gdn, z_gdn_conv, p, layers)
    y_s, s_ssd, s_ssd_conv, s_gdn, s_gdn_conv, s_gmlp_v = _trunk(
        x_sample, mod[:, bp:], state_ssd, state_ssd_conv, state_gdn, state_gdn_conv, p, layers)
    return (y_p, y_s, p_ssd, p_ssd_conv, p_gdn, p_gdn_conv, s_ssd, s_ssd_conv, s_gdn, s_gdn_conv, s_gmlp_v)
```

```python
import functools

import jax
import jax.numpy as jnp
from jax import lax
from jax.experimental import pallas as pl
from jax.experimental.pallas import tpu as pltpu

F32 = jnp.float32
BF16 = jnp.bfloat16
I32 = jnp.int32

LANES = 128
SUBLANES = 8
VMEM_LIMIT_BYTES = 56 * 1024 * 1024

D_MODEL = 1024
EPS = 1e-6
CHUNK = 64
CONV_W = 4
SSD_HEADS = 6
SSD_HEAD_DIM = 64
SSD_STATE = 64
SSD_GROUPS = 2
SSD_D = SSD_HEADS * SSD_HEAD_DIM
SSD_CONV_DIM = SSD_D + 2 * SSD_GROUPS * SSD_STATE
GMLP_GROUPS = 4
GMLP_GROUP_DIM = 64
GMLP_D = GMLP_GROUPS * GMLP_GROUP_DIM
GMLP_CHUNK = 128
GDN_HEADS = 6
GDN_KEY_DIM = 64
GDN_VAL_DIM = 64
GDN_QK_D = GDN_HEADS * GDN_KEY_DIM
GDN_V_D = GDN_HEADS * GDN_VAL_DIM
GDN_CONV_DIM = 2 * GDN_QK_D + GDN_V_D
GDN_PAIRS = GDN_HEADS // 2
SSD_PAIRS = SSD_HEADS // 2
assert 2 * SSD_HEAD_DIM == 2 * SSD_STATE == 2 * GDN_KEY_DIM == 2 * GDN_VAL_DIM == 128 and SSD_PAIRS == GDN_PAIRS
N_EXPERTS = 16
N_EXPERT_GROUPS = 4
EXPERTS_PER_GROUP = N_EXPERTS // N_EXPERT_GROUPS
D_EXPERT = 512

SEG_WIDTHS = (SSD_CONV_DIM, SSD_D, GMLP_D, GMLP_D, GDN_CONV_DIM, GDN_V_D, LANES)
IN_R = sum(SEG_WIDTHS)
SSD_DT_LANE = 0
GDN_A_LANE = 8
GDN_B_LANE = 16
TAIL_ROW = SUBLANES - (CONV_W - 1)

MOE_BLOCK = 256
ROW_TILE = 256
POS_TILE = 2048
GDN_STEP_CHUNKS = 2
SSD_STEP_CHUNKS = 2
DMA_TILE = 512
DMA_UNROLL = 8


def _cparams(*sem):
    return pltpu.CompilerParams(dimension_semantics=sem, vmem_limit_bytes=VMEM_LIMIT_BYTES)


def _bdot(a, b):
    return jnp.dot(a.astype(BF16), b.astype(BF16), preferred_element_type=F32)


def _bdot_nt(a, b):
    return lax.dot_general(a.astype(BF16), b.astype(BF16), (((1,), (1,)), ((), ())), preferred_element_type=F32)


def _bdot_tn(a, b):
    return lax.dot_general(a.astype(BF16), b.astype(BF16), (((0,), (0,)), ((), ())), preferred_element_type=F32)


def _sigmoid(x):
    return 1.0 / (1.0 + jnp.exp(-x))


def _silu(x):
    return x * _sigmoid(x)


def _softplus(x):
    return jnp.maximum(x, 0.0) + jnp.log1p(jnp.exp(-jnp.abs(x)))


def _gelu_tanh(x):
    return 0.5 * x * (1.0 + jnp.tanh(0.7978845608028654 * (x + 0.044715 * (x * x * x))))


def _cumsum_rows(a):
    n = a.shape[0]
    row = lax.broadcasted_iota(I32, a.shape, 0)
    s = 1
    while s < n:
        a = a + jnp.where(row >= s, pltpu.roll(a, s, axis=0), jnp.zeros_like(a))
        s *= 2
    return a


def _cumsum_lanes(a):
    n = a.shape[1]
    col = lax.broadcasted_iota(I32, a.shape, 1)
    s = 1
    while s < n:
        a = a + jnp.where(col >= s, pltpu.roll(a, s, axis=1), jnp.zeros_like(a))
        s *= 2
    return a


def _rows_to_lanes(a):
    q = a.shape[0]
    if q < LANES:
        a = jnp.concatenate([a, jnp.zeros((LANES - q, LANES), a.dtype)], axis=0)
    return a.T[:, :q]


def _ada_body(c_ref, w_ref, b_ref, o_ref):
    o_ref[0] = _bdot(_silu(c_ref[...]), w_ref[0]) + b_ref[0]


def _ada_mod(c_all, w_ada, b_ada):
    depth, d, n6 = w_ada.shape
    bc = c_all.shape[0]
    tn = 1536
    return pl.pallas_call(
        _ada_body,
        grid=(depth, n6 // tn),
        in_specs=[pl.BlockSpec((bc, d), lambda l, j: (0, 0)),
                  pl.BlockSpec((1, d, tn), lambda l, j: (l, 0, j)),
                  pl.BlockSpec((1, 1, tn), lambda l, j: (l, 0, j))],
        out_specs=pl.BlockSpec((1, bc, tn), lambda l, j: (l, 0, j)),
        out_shape=jax.ShapeDtypeStruct((depth, bc, n6), F32),
        compiler_params=_cparams("parallel", "parallel"),
        name="ada_mod",
    )(c_all, w_ada, b_ada.reshape(depth, 1, n6))


def _seq_block(tm, seq_len, d):
    if tm <= seq_len:
        per_seq = seq_len // tm
        return pl.BlockSpec((1, 1, d), lambda i: (i // per_seq, 0, 0))
    return pl.BlockSpec((tm // seq_len, 1, d), lambda i: (i, 0, 0))


def _seq_rows(ref, tm):
    s = ref.shape[0]
    if s == 1:
        return ref[0]
    return jnp.concatenate([jnp.broadcast_to(ref[k], (tm // s, ref.shape[2])) for k in range(s)], axis=0)


def _mod_rms(x, nw, sc, sh):
    ms = jnp.mean(x * x, axis=-1, keepdims=True)
    return (x * lax.rsqrt(ms + EPS) * nw) * (1.0 + sc) + sh


def _inproj_body(x_ref, sh_ref, sc_ref, nw_ref, w_ref, *out_refs):
    tm = x_ref.shape[0]
    hb = _mod_rms(x_ref[...], nw_ref[...], _seq_rows(sc_ref, tm), _seq_rows(sh_ref, tm)).astype(BF16)
    off = 0
    for ref, width in zip(out_refs, SEG_WIDTHS):
        ref[...] = jnp.dot(hb, w_ref[:, off:off + width], preferred_element_type=F32)
        off += width


def _inproj(x2, sh, sc, nw, w_r, seq_len):
    n, d = x2.shape
    tm = min(ROW_TILE, n)
    row = lambda i: (i, 0)
    return pl.pallas_call(
        _inproj_body,
        grid=(n // tm,),
        in_specs=[pl.BlockSpec((tm, d), row),
                  _seq_block(tm, seq_len, d),
                  _seq_block(tm, seq_len, d),
                  pl.BlockSpec((1, d), lambda i: (0, 0)),
                  pl.BlockSpec((d, IN_R), lambda i: (0, 0))],
        out_specs=[pl.BlockSpec((tm, w), row) for w in SEG_WIDTHS],
        out_shape=[jax.ShapeDtypeStruct((n, w), F32) for w in SEG_WIDTHS],
        compiler_params=_cparams("parallel"),
        name="inproj",
    )(x2, sh, sc, nw, w_r)


def _causal_conv_step(x_ref, buf0_ref, bufout_ref, xp_ref, cw_ref, first):
    q = x_ref.shape[0]

    @pl.when(first)
    def _():
        xp_ref[TAIL_ROW:SUBLANES, :] = buf0_ref[0]

    xp_ref[SUBLANES:SUBLANES + q, :] = x_ref[...]
    acc = cw_ref[0:1, :] * xp_ref[TAIL_ROW:TAIL_ROW + q, :]
    for k in range(1, CONV_W):
        acc = acc + cw_ref[k:k + 1, :] * xp_ref[TAIL_ROW + k:TAIL_ROW + k + q, :]
    tail = xp_ref[TAIL_ROW + q:SUBLANES + q, :]
    bufout_ref[0] = tail
    xp_ref[TAIL_ROW:SUBLANES, :] = tail
    return acc


def _decay_terms(g):
    q = g.shape[0]
    cum = _cumsum_rows(g)
    cum_t = _rows_to_lanes(cum)
    last = cum[q - 1:q, :]
    return cum, cum_t, jnp.exp(cum), jnp.exp(last - cum), jnp.exp(last)


def _ssd_body(xbc_ref, z_ref, sm_ref, h0_ref, buf0_ref, cw_ref, cb_ref, dtb_ref, alog_ref, dvec_ref, nw_ref,
              y_ref, hout_ref, bufout_ref, xp_ref, ht_ref):
    t = xbc_ref.shape[0]
    q = min(CHUNK, t)
    first = pl.program_id(1) == 0

    @pl.when(first)
    def _():
        ht_ref[...] = h0_ref[0]

    xa = _silu(_causal_conv_step(xbc_ref, buf0_ref, bufout_ref, xp_ref, cw_ref, first) + cb_ref[...])
    dt = _softplus(sm_ref[...] + dtb_ref[...])
    a_all = dt * (-jnp.exp(alog_ref[...]))
    causal = lax.broadcasted_iota(I32, (q, LANES), 0) >= (lax.broadcasted_iota(I32, (q, LANES), 1) & (q - 1))
    lo = _low_half((q, LANES))
    b_off, c_off = SSD_D, SSD_D + SSD_GROUPS * SSD_STATE
    y_loc, st, ce, ed = {}, {}, {}, {}
    for c in range(t // q):
        rows = slice(c * q, (c + 1) * q)
        cum = _cumsum_rows(a_all[rows])
        cum_rows = jnp.concatenate([cum, pltpu.roll(cum, LANES - 1, axis=1)], axis=0).T
        last = cum[q - 1:q, :]
        ecum, wend, elast = jnp.exp(cum), jnp.exp(last - cum), jnp.exp(last)
        bm = xa[rows, b_off:b_off + LANES]
        cm = xa[rows, c_off:c_off + LANES]
        bm_sw = pltpu.roll(bm, SSD_STATE, axis=1)
        cm_sw = pltpu.roll(cm, SSD_STATE, axis=1)
        b_pairs = [jnp.where(lo, bm, bm_sw), bm, jnp.where(lo, bm_sw, bm)]
        c_pairs = [jnp.where(lo, cm, cm_sw), cm, jnp.where(lo, cm_sw, cm)]
        for p in range(SSD_PAIRS):
            lane = SSD_DT_LANE + 2 * p
            ls = slice(p * LANES, (p + 1) * LANES)
            xh = xa[rows, ls]
            xdt = xh * _pair_cols(dt[rows], lane)
            e = jnp.exp(_pair_cols(cum, lane) - cum_rows[lane:lane + 1, :])
            cb = lax.dot_general(c_pairs[p].astype(BF16), _block_diag(b_pairs[p].astype(BF16)),
                                 (((1,), (1,)), ((), ())), preferred_element_type=F32)
            y_loc[c, p] = _pdot(cb * jnp.where(causal, e, 0.0), xdt) + dvec_ref[:, ls] * xh
            st[c, p] = _bdot_tn(b_pairs[p] * _pair_cols(wend, lane), xdt)
            ce[c, p] = c_pairs[p] * _pair_cols(ecum, lane)
            ed[c, p] = _pair_cols(elast, lane)
    h_cur = [ht_ref[p] for p in range(SSD_PAIRS)]
    half = SSD_D // SSD_GROUPS
    for c in range(t // q):
        rows = slice(c * q, (c + 1) * q)
        yz = []
        for p in range(SSD_PAIRS):
            ls = slice(p * LANES, (p + 1) * LANES)
            y = y_loc[c, p] + _pdot(ce[c, p], h_cur[p])
            h_cur[p] = h_cur[p] * ed[c, p] + jnp.where(lo, st[c, p][:q], st[c, p][q:])
            yz.append(y * _silu(z_ref[rows, ls]))
        sq = [v * v for v in yz]
        row_sum = lambda v: jnp.sum(v, axis=-1, keepdims=True)
        ss0 = row_sum(sq[0]) + row_sum(jnp.where(lo, sq[1], 0.0))
        ss1 = row_sum(jnp.where(lo, 0.0, sq[1])) + row_sum(sq[2])
        r0 = lax.rsqrt(ss0 * (1.0 / half) + EPS)
        r1 = lax.rsqrt(ss1 * (1.0 / half) + EPS)
        for p, scale in enumerate((r0, jnp.where(lo, r0, r1), r1)):
            ls = slice(p * LANES, (p + 1) * LANES)
            y_ref[rows, ls] = yz[p] * scale * nw_ref[:, ls]
    for p in range(SSD_PAIRS):
        ht_ref[p] = h_cur[p]
        hout_ref[0, p] = h_cur[p]


def _ssd(xbc, z, sm, h0t, buf0, cw, cb, dtb, alog, dvec, nw, bsz, seq_len):
    t = min(CHUNK * SSD_STEP_CHUNKS, seq_len)
    nc = seq_len // t
    n = bsz * seq_len
    tok = lambda b, c: (b * nc + c, 0)
    per_b4 = lambda b, c: (b, 0, 0, 0)
    per_b3 = lambda b, c: (b, 0, 0)
    const = lambda b, c: (0, 0)
    state = (SSD_PAIRS, SSD_STATE, 2 * SSD_HEAD_DIM)
    y, h_pairs, buf = pl.pallas_call(
        _ssd_body,
        grid=(bsz, nc),
        in_specs=[pl.BlockSpec((t, SSD_CONV_DIM), tok),
                  pl.BlockSpec((t, SSD_D), tok),
                  pl.BlockSpec((t, LANES), tok),
                  pl.BlockSpec((1,) + state, per_b4),
                  pl.BlockSpec((1, CONV_W - 1, SSD_CONV_DIM), per_b3),
                  pl.BlockSpec((CONV_W, SSD_CONV_DIM), const),
                  pl.BlockSpec((1, SSD_CONV_DIM), const),
                  pl.BlockSpec((1, LANES), const),
                  pl.BlockSpec((1, LANES), const),
                  pl.BlockSpec((1, SSD_D), const),
                  pl.BlockSpec((1, SSD_D), const)],
        out_specs=[pl.BlockSpec((t, SSD_D), tok),
                   pl.BlockSpec((1,) + state, per_b4),
                   pl.BlockSpec((1, CONV_W - 1, SSD_CONV_DIM), per_b3)],
        out_shape=[jax.ShapeDtypeStruct((n, SSD_D), F32),
                   jax.ShapeDtypeStruct((bsz,) + state, F32),
                   jax.ShapeDtypeStruct((bsz, CONV_W - 1, SSD_CONV_DIM), F32)],
        scratch_shapes=[pltpu.VMEM((SUBLANES + t, SSD_CONV_DIM), F32),
                        pltpu.VMEM(state, F32)],
        compiler_params=_cparams("parallel", "arbitrary"),
        name="ssd",
    )(xbc, z, sm, _heads_to_pairs(h0t), buf0, cw, cb, dtb, alog, dvec, nw)
    return y, _pairs_to_heads(h_pairs), buf


def _gmlp_body(u_ref, v_ref, lng_ref, lnb_ref, ws_ref, bst_ref, y_ref, vn_ref):
    q = u_ref.shape[0]
    gu = _gelu_tanh(u_ref[...])
    gv = _gelu_tanh(v_ref[...])
    lower = lax.broadcasted_iota(I32, (q, q), 0) >= lax.broadcasted_iota(I32, (q, q), 1)
    for g in range(GMLP_GROUPS):
        gs = slice(g * GMLP_GROUP_DIM, (g + 1) * GMLP_GROUP_DIM)
        vg = gv[:, gs]
        mu = jnp.mean(vg, axis=-1, keepdims=True)
        dv = vg - mu
        var = jnp.mean(dv * dv, axis=-1, keepdims=True)
        vn = dv * lax.rsqrt(var + EPS) * lng_ref[:, gs] + lnb_ref[:, gs]
        vn_ref[:, gs] = vn
        mixed = _bdot(jnp.where(lower, ws_ref[g], 0.0), vn) + bst_ref[:, g:g + 1]
        y_ref[:, gs] = gu[:, gs] * mixed


def _gmlp(u, v, lng, lnb, ws, bst, bsz, seq_len):
    q = min(GMLP_CHUNK, seq_len)
    n = bsz * seq_len
    tok = lambda i: (i, 0)
    return pl.pallas_call(
        _gmlp_body,
        grid=(n // q,),
        in_specs=[pl.BlockSpec((q, GMLP_D), tok),
                  pl.BlockSpec((q, GMLP_D), tok),
                  pl.BlockSpec((1, GMLP_D), lambda i: (0, 0)),
                  pl.BlockSpec((1, GMLP_D), lambda i: (0, 0)),
                  pl.BlockSpec((GMLP_GROUPS, q, q), lambda i: (0, 0, 0)),
                  pl.BlockSpec((q, GMLP_GROUPS), lambda i: (0, 0))],
        out_specs=[pl.BlockSpec((q, GMLP_D), tok), pl.BlockSpec((q, GMLP_D), tok)],
        out_shape=[jax.ShapeDtypeStruct((n, GMLP_D), F32), jax.ShapeDtypeStruct((n, GMLP_D), F32)],
        compiler_params=_cparams("parallel"),
        name="gmlp",
    )(u, v, lng, lnb, ws, bst)


def _low_half(shape):
    return lax.broadcasted_iota(I32, shape, 1) < GDN_KEY_DIM


def _pair_cols(m, lane):
    shape = (m.shape[0], LANES)
    return jnp.where(_low_half(shape), m[:, lane:lane + 1], m[:, lane + 1:lane + 2])


def _half_sums(s):
    lo = _low_half(s.shape)
    s_lo = jnp.sum(jnp.where(lo, s, 0.0), axis=-1, keepdims=True)
    s_hi = jnp.sum(jnp.where(lo, 0.0, s), axis=-1, keepdims=True)
    return jnp.where(lo, s_lo, s_hi)


def _l2_halves(x):
    return x * lax.rsqrt(_half_sums(x * x) + EPS)


def _block_diag(y):
    q = y.shape[0]
    yy = jnp.concatenate([y, y], axis=0)
    same = (lax.broadcasted_iota(I32, yy.shape, 0) < q) == _low_half(yy.shape)
    return jnp.where(same, yy, jnp.zeros_like(yy))


def _pdot(x, y):
    return jnp.dot(x.astype(BF16), _block_diag(y.astype(BF16)), preferred_element_type=F32)


def _unit_lower_inverse_minus_identity(mats):
    q = mats[0].shape[0]
    ii = lax.broadcasted_iota(I32, (q, LANES), 0)
    jj = lax.broadcasted_iota(I32, (q, LANES), 1) & (q - 1)
    base = (ii >> 1) == (jj >> 1)
    rs = [-jnp.where(base, a, 0.0) for a in mats]
    level = 1
    while (1 << level) < q:
        in_pair = (ii >> (level + 1)) == (jj >> (level + 1))
        off_diag = (ii >> level) != (jj >> level)
        mask = jnp.logical_and(in_pair, off_diag)
        ams = [jnp.where(mask, a, 0.0) for a in mats]
        ws = [am + _pdot(am, r) for am, r in zip(ams, rs)]
        rs = [r - w - _pdot(r, w) for r, w in zip(rs, ws)]
        level += 1
    return rs


def _gdn_body(qkv_ref, gg_ref, sm_ref, s0_ref, buf0_ref, cw_ref, dtb_ref, alog_ref, nw_ref,
              y_ref, sout_ref, bufout_ref, xp_ref, s_ref):
    t = qkv_ref.shape[0]
    q = min(CHUNK, t)
    first = pl.program_id(1) == 0

    @pl.when(first)
    def _():
        s_ref[...] = s0_ref[0]

    xa = _silu(_causal_conv_step(qkv_ref, buf0_ref, bufout_ref, xp_ref, cw_ref, first))
    sm = sm_ref[...]
    beta_all = _sigmoid(sm)
    g_all = -jnp.exp(alog_ref[...]) * _softplus(sm + dtb_ref[...])
    ii = lax.broadcasted_iota(I32, (q, LANES), 0)
    jj = lax.broadcasted_iota(I32, (q, LANES), 1) & (q - 1)
    lo = _low_half((q, LANES))
    chains = [(c, p) for c in range(t // q) for p in range(GDN_PAIRS)]
    qp, kp, kdec, edec, kk, a_mat, attn, vb, kbe, qe = {}, {}, {}, {}, {}, {}, {}, {}, {}, {}
    for c in range(t // q):
        rows = slice(c * q, (c + 1) * q)
        cum = _cumsum_rows(g_all[rows])
        cum_rows = jnp.concatenate([cum, pltpu.roll(cum, LANES - 1, axis=1)], axis=0).T
        last = cum[q - 1:q, :]
        ecum, kdecw, elast = jnp.exp(cum), jnp.exp(last - cum), jnp.exp(last)
        for p in range(GDN_PAIRS):
            a_lane, b_lane = GDN_A_LANE + 2 * p, GDN_B_LANE + 2 * p
            ls = slice(p * LANES, (p + 1) * LANES)
            qp[c, p] = _l2_halves(xa[rows, ls]) * (GDN_KEY_DIM ** -0.5)
            kp[c, p] = _l2_halves(xa[rows, GDN_QK_D + p * LANES:GDN_QK_D + (p + 1) * LANES])
            beta = _pair_cols(beta_all[rows], b_lane)
            kb = kp[c, p] * beta
            e = jnp.exp(_pair_cols(cum, a_lane) - cum_rows[a_lane:a_lane + 1, :])
            ecp = _pair_cols(ecum, a_lane)
            kdec[c, p] = kp[c, p] * _pair_cols(kdecw, a_lane)
            edec[c, p] = _pair_cols(elast, a_lane)
            kk[c, p] = lax.dot_general(jnp.concatenate([kb, qp[c, p]], axis=0).astype(BF16),
                                       _block_diag(kp[c, p].astype(BF16)),
                                       (((1,), (1,)), ((), ())), preferred_element_type=F32)
            a_mat[c, p] = kk[c, p][:q] * jnp.where(ii > jj, e, 0.0)
            attn[c, p] = kk[c, p][q:] * jnp.where(ii >= jj, e, 0.0)
            vb[c, p] = xa[rows, 2 * GDN_QK_D + p * LANES:2 * GDN_QK_D + (p + 1) * LANES] * beta
            kbe[c, p] = kb * ecp
            qe[c, p] = qp[c, p] * ecp
    r = dict(zip(chains, _unit_lower_inverse_minus_identity([a_mat[cp] for cp in chains])))
    u_base = {cp: vb[cp] + _pdot(r[cp], vb[cp]) for cp in chains}
    k_cd = {cp: kbe[cp] + _pdot(r[cp], kbe[cp]) for cp in chains}
    s_cur = [s_ref[p] for p in range(GDN_PAIRS)]
    for c in range(t // q):
        rows = slice(c * q, (c + 1) * q)
        ks = [jnp.dot(jnp.concatenate([k_cd[c, p], qe[c, p]], axis=0).astype(BF16),
                      _block_diag(s_cur[p].astype(BF16)), preferred_element_type=F32) for p in range(GDN_PAIRS)]
        u = [u_base[c, p] - ks[p][:q] for p in range(GDN_PAIRS)]
        au = [_pdot(attn[c, p], u[p]) for p in range(GDN_PAIRS)]
        ku = [_bdot_tn(kdec[c, p], u[p]) for p in range(GDN_PAIRS)]
        for p in range(GDN_PAIRS):
            s_cur[p] = s_cur[p] * edec[c, p] + jnp.where(lo, ku[p][:q], ku[p][q:])
            o = ks[p][q:] + au[p]
            o = o * lax.rsqrt(_half_sums(o * o) * (1.0 / GDN_VAL_DIM) + EPS) * nw_ref[...]
            ls = slice(p * LANES, (p + 1) * LANES)
            y_ref[rows, ls] = o * _silu(gg_ref[rows, ls])
    for p in range(GDN_PAIRS):
        s_ref[p] = s_cur[p]
        sout_ref[0, p] = s_cur[p]


def _heads_to_pairs(s):
    b = s.shape[0]
    s = s.reshape(b, GDN_PAIRS, 2, GDN_KEY_DIM, GDN_VAL_DIM)
    return jnp.swapaxes(s, 2, 3).reshape(b, GDN_PAIRS, GDN_KEY_DIM, 2 * GDN_VAL_DIM)


def _pairs_to_heads(s):
    b = s.shape[0]
    s = s.reshape(b, GDN_PAIRS, GDN_KEY_DIM, 2, GDN_VAL_DIM)
    return jnp.swapaxes(s, 2, 3).reshape(b, GDN_HEADS, GDN_KEY_DIM, GDN_VAL_DIM)


def _gdn(qkv, gg, sm, s0, buf0, cw, dtb, alog, nw, bsz, seq_len):
    t = min(CHUNK * GDN_STEP_CHUNKS, seq_len)
    nc = seq_len // t
    n = bsz * seq_len
    tok = lambda b, c: (b * nc + c, 0)
    per_b4 = lambda b, c: (b, 0, 0, 0)
    per_b3 = lambda b, c: (b, 0, 0)
    const = lambda b, c: (0, 0)
    state = (GDN_PAIRS, GDN_KEY_DIM, 2 * GDN_VAL_DIM)
    y, s_pairs, buf = pl.pallas_call(
        _gdn_body,
        grid=(bsz, nc),
        in_specs=[pl.BlockSpec((t, GDN_CONV_DIM), tok),
                  pl.BlockSpec((t, GDN_V_D), tok),
                  pl.BlockSpec((t, LANES), tok),
                  pl.BlockSpec((1,) + state, per_b4),
                  pl.BlockSpec((1, CONV_W - 1, GDN_CONV_DIM), per_b3),
                  pl.BlockSpec((CONV_W, GDN_CONV_DIM), const),
                  pl.BlockSpec((1, LANES), const),
                  pl.BlockSpec((1, LANES), const),
                  pl.BlockSpec((1, 2 * GDN_VAL_DIM), const)],
        out_specs=[pl.BlockSpec((t, GDN_V_D), tok),
                   pl.BlockSpec((1,) + state, per_b4),
                   pl.BlockSpec((1, CONV_W - 1, GDN_CONV_DIM), per_b3)],
        out_shape=[jax.ShapeDtypeStruct((n, GDN_V_D), F32),
                   jax.ShapeDtypeStruct((bsz,) + state, F32),
                   jax.ShapeDtypeStruct((bsz, CONV_W - 1, GDN_CONV_DIM), F32)],
        scratch_shapes=[pltpu.VMEM((SUBLANES + t, GDN_CONV_DIM), F32),
                        pltpu.VMEM(state, F32)],
        compiler_params=_cparams("parallel", "arbitrary"),
        name="gdn",
    )(qkv, gg, sm, _heads_to_pairs(s0), buf0, cw, dtb, alog, jnp.concatenate([nw, nw], axis=1))
    return y, _pairs_to_heads(s_pairs), buf


def _route(logits, b_col):
    aff = _sigmoid(logits)
    sel = aff + b_col
    srow = [sel[e:e + 1, :] for e in range(N_EXPERTS)]
    arow = [aff[e:e + 1, :] for e in range(N_EXPERTS)]
    gscore = []
    for g in range(N_EXPERT_GROUPS):
        a, b, c, d = srow[g * EXPERTS_PER_GROUP:(g + 1) * EXPERTS_PER_GROUP]
        hi1, lo1, hi2, lo2 = jnp.maximum(a, b), jnp.minimum(a, b), jnp.maximum(c, d), jnp.minimum(c, d)
        gscore.append(jnp.maximum(hi1, hi2) + jnp.maximum(jnp.minimum(hi1, hi2), jnp.maximum(lo1, lo2)))
    best = jnp.zeros(gscore[0].shape, I32)
    best_v = gscore[0]
    for g in range(1, N_EXPERT_GROUPS):
        upd = gscore[g] > best_v
        best = jnp.where(upd, g, best)
        best_v = jnp.where(upd, gscore[g], best_v)
    cs, ca = [], []
    for j in range(EXPERTS_PER_GROUP):
        cj, aj = srow[j], arow[j]
        for g in range(1, N_EXPERT_GROUPS):
            m = best == g
            cj = jnp.where(m, srow[g * EXPERTS_PER_GROUP + j], cj)
            aj = jnp.where(m, arow[g * EXPERTS_PER_GROUP + j], aj)
        cs.append(cj)
        ca.append(aj)
    i1 = jnp.zeros(best.shape, I32)
    v1, a1 = cs[0], ca[0]
    for j in range(1, EXPERTS_PER_GROUP):
        upd = cs[j] > v1
        i1 = jnp.where(upd, j, i1)
        v1 = jnp.where(upd, cs[j], v1)
        a1 = jnp.where(upd, ca[j], a1)
    i2 = jnp.zeros(best.shape, I32)
    v2 = jnp.full(v1.shape, -jnp.inf, F32)
    a2 = jnp.zeros(v1.shape, F32)
    for j in range(EXPERTS_PER_GROUP):
        upd = jnp.logical_and(i1 != j, cs[j] > v2)
        i2 = jnp.where(upd, j, i2)
        v2 = jnp.where(upd, cs[j], v2)
        a2 = jnp.where(upd, ca[j], a2)
    tot = a1 + a2
    return best * EXPERTS_PER_GROUP + i1, best * EXPERTS_PER_GROUP + i2, a1 / tot, a2 / tot


def _outproj_body(x_ref, ya_ref, yb_ref, yc_ref, wa_ref, wb_ref, wc_ref, ga_ref, nw_ref, sc_ref, sh_ref,
                  wrt_ref, br_ref, xo_ref, h2_ref, eidx_ref, wt_ref):
    m = _bdot(ya_ref[...], wa_ref[...]) + _bdot(yb_ref[...], wb_ref[...]) + _bdot(yc_ref[...], wc_ref[...])
    tm = x_ref.shape[0]
    x = x_ref[...] + _seq_rows(ga_ref, tm) * m
    xo_ref[...] = x
    h2 = _mod_rms(x, nw_ref[...], _seq_rows(sc_ref, tm), _seq_rows(sh_ref, tm))
    h2_ref[...] = h2
    e1, e2, w1, w2 = _route(_bdot_nt(wrt_ref[...], h2), br_ref[...])
    eidx_ref[0:1, :] = e1
    eidx_ref[1:2, :] = e2
    row = lax.broadcasted_iota(I32, (LANES, w1.shape[1]), 0)
    wt_ref[...] = jnp.where(row == 0, w1, jnp.where(row == 1, w2, 0.0)).T


def _outproj_route(x2, ya, yb, yc, wa, wb, wc, ga1, nw, sc2, sh2, wrt, br, seq_len):
    n, d = x2.shape
    tm = min(ROW_TILE, n)
    row = lambda i: (i, 0)
    seq = _seq_block(tm, seq_len, d)
    const = lambda i: (0, 0)
    return pl.pallas_call(
        _outproj_body,
        grid=(n // tm,),
        in_specs=[pl.BlockSpec((tm, d), row),
                  pl.BlockSpec((tm, SSD_D), row),
                  pl.BlockSpec((tm, GMLP_D), row),
                  pl.BlockSpec((tm, GDN_V_D), row),
                  pl.BlockSpec((SSD_D, d), const),
                  pl.BlockSpec((GMLP_D, d), const),
                  pl.BlockSpec((GDN_V_D, d), const),
                  seq,
                  pl.BlockSpec((1, d), const),
                  seq,
                  seq,
                  pl.BlockSpec((N_EXPERTS, d), const),
                  pl.BlockSpec((N_EXPERTS, 1), const)],
        out_specs=[pl.BlockSpec((tm, d), row),
                   pl.BlockSpec((tm, d), row),
                   pl.BlockSpec((2, tm), lambda i: (0, i)),
                   pl.BlockSpec((tm, LANES), row)],
        out_shape=[jax.ShapeDtypeStruct((n, d), F32),
                   jax.ShapeDtypeStruct((n, d), F32),
                   jax.ShapeDtypeStruct((2, n), I32),
                   jax.ShapeDtypeStruct((n, LANES), F32)],
        compiler_params=_cparams("parallel"),
        name="outproj_route",
    )(x2, ya, yb, yc, wa, wb, wc, ga1, nw, sc2, sh2, wrt, br)


def _positions_body(eidx_ref, dest_ref, blk_ref, nblk_ref, zrow_ref, cnt_ref, run_ref, *, nb_total):
    phase = pl.program_id(0)
    i = pl.program_id(1)
    t = eidx_ref.shape[1]
    e_iota = lax.broadcasted_iota(I32, (N_EXPERTS, t), 0)
    m0 = e_iota == eidx_ref[0:1, :]
    m1 = e_iota == eidx_ref[1:2, :]
    hits = jnp.where(jnp.logical_or(m0, m1), 1.0, 0.0)
    tile_cnt = jnp.broadcast_to(jnp.sum(hits, axis=1, keepdims=True), (N_EXPERTS, LANES))

    @pl.when(jnp.logical_and(phase == 0, i == 0))
    def _():
        cnt_ref[...] = jnp.zeros_like(cnt_ref)

    @pl.when(phase == 0)
    def _():
        cnt_ref[...] += tile_cnt

    @pl.when(jnp.logical_and(phase == 1, i == 0))
    def _():
        padded = jnp.floor((cnt_ref[...] + (MOE_BLOCK - 1)) * (1.0 / MOE_BLOCK)) * MOE_BLOCK
        pad_end = _cumsum_rows(padded)
        run_ref[...] = pad_end - padded
        nb = blk_ref.shape[1]
        blk_start = (lax.broadcasted_iota(I32, (N_EXPERTS, nb), 1) * MOE_BLOCK).astype(F32)
        past = jnp.where(blk_start >= pad_end[:, 0:1], 1.0, 0.0)
        blk_ref[...] = jnp.minimum(jnp.sum(past, axis=0, keepdims=True), N_EXPERTS - 1.0).astype(I32)
        nblk = pad_end[N_EXPERTS - 1:N_EXPERTS, :] * (1.0 / MOE_BLOCK)
        nblk_ref[...] = nblk.astype(I32)
        row = lax.broadcasted_iota(I32, (N_EXPERTS, LANES), 0)
        lane = lax.broadcasted_iota(I32, (N_EXPERTS, LANES), 1)
        last_blk = jnp.where(padded > 0.0, pad_end - MOE_BLOCK, -1.0)
        per_expert = jnp.sum(jnp.where(row == lane, last_blk, 0.0), axis=0, keepdims=True)
        lane1 = lane[0:1, :]
        tail_blk = nblk + (lane1 - N_EXPERTS).astype(F32)
        tail = jnp.where(tail_blk < nb_total, tail_blk * MOE_BLOCK, -1.0)
        zrow = jnp.where(lane1 < N_EXPERTS, per_expert, jnp.where(lane1 < 2 * N_EXPERTS, tail, -1.0))
        zrow_ref[...] = zrow.astype(I32)

    @pl.when(phase == 1)
    def _():
        excl = _cumsum_lanes(hits) - hits
        pos = run_ref[:, 0:1] + excl
        dest_ref[0:1, :] = jnp.sum(jnp.where(m0, pos, 0.0), axis=0, keepdims=True).astype(I32)
        dest_ref[1:2, :] = jnp.sum(jnp.where(m1, pos, 0.0), axis=0, keepdims=True).astype(I32)
        run_ref[...] += tile_cnt


def _positions(eidx, n_blocks):
    n = eidx.shape[1]
    t = min(POS_TILE, n)
    nb_pad = -(-n_blocks // LANES) * LANES
    return pl.pallas_call(
        functools.partial(_positions_body, nb_total=n_blocks),
        grid=(2, n // t),
        in_specs=[pl.BlockSpec((2, t), lambda p, i: (0, i))],
        out_specs=[pl.BlockSpec((2, t), lambda p, i: (0, i * p)),
                   pl.BlockSpec((1, nb_pad), lambda p, i: (0, 0)),
                   pl.BlockSpec((1, LANES), lambda p, i: (0, 0)),
                   pl.BlockSpec((1, LANES), lambda p, i: (0, 0))],
        out_shape=[jax.ShapeDtypeStruct((2, n), I32),
                   jax.ShapeDtypeStruct((1, nb_pad), I32),
                   jax.ShapeDtypeStruct((1, LANES), I32),
                   jax.ShapeDtypeStruct((1, LANES), I32)],
        scratch_shapes=[pltpu.VMEM((N_EXPERTS, LANES), F32), pltpu.VMEM((N_EXPERTS, LANES), F32)],
        compiler_params=_cparams("arbitrary", "arbitrary"),
        name="moe_positions",
    )(eidx)


def _row_copy(src_ref, src_row, dst_ref, dst_row, sem):
    return pltpu.make_async_copy(src_ref.at[pl.ds(src_row, 1), :], dst_ref.at[pl.ds(dst_row, 1), :], sem)


def _dispatch_body(dest_ref, zrow_ref, h_ref, xs_ref, zero_ref, sem, zsem):
    t = dest_ref.shape[1]

    @pl.when(pl.program_id(0) == 0)
    def _():
        zero_ref[...] = jnp.zeros_like(zero_ref)

        def zero_copy(k):
            start = pl.multiple_of(zrow_ref[0, k], MOE_BLOCK)
            return pltpu.make_async_copy(zero_ref, xs_ref.at[pl.ds(start, MOE_BLOCK), :], zsem)

        for k in range(2 * N_EXPERTS):
            @pl.when(zrow_ref[0, k] >= 0)
            def _():
                zero_copy(k).start()

        for k in range(2 * N_EXPERTS):
            @pl.when(zrow_ref[0, k] >= 0)
            def _():
                zero_copy(k).wait()

    def issue(jo, carry):
        for k in range(DMA_UNROLL):
            j = jo * DMA_UNROLL + k
            _row_copy(h_ref, j, xs_ref, dest_ref[0, j], sem.at[0]).start()
            _row_copy(h_ref, j, xs_ref, dest_ref[1, j], sem.at[1]).start()
        return carry

    lax.fori_loop(0, t // DMA_UNROLL, issue, 0)
    for k in range(2):
        pltpu.make_async_copy(h_ref, xs_ref.at[pl.ds(0, t), :], sem.at[k]).wait()


def _dispatch(dest, zrow, h2, n_rows):
    n, d = h2.shape
    t = min(DMA_TILE, n)
    return pl.pallas_call(
        _dispatch_body,
        grid=(n // t,),
        in_specs=[pl.BlockSpec((2, t), lambda i: (0, i), memory_space=pltpu.SMEM),
                  pl.BlockSpec((1, LANES), lambda i: (0, 0), memory_space=pltpu.SMEM),
                  pl.BlockSpec((t, d), lambda i: (i, 0))],
        out_specs=pl.BlockSpec(memory_space=pl.ANY),
        out_shape=jax.ShapeDtypeStruct((n_rows, d), F32),
        scratch_shapes=[pltpu.VMEM((MOE_BLOCK, d), F32), pltpu.SemaphoreType.DMA((2,)), pltpu.SemaphoreType.DMA(())],
        compiler_params=_cparams("arbitrary"),
        name="moe_dispatch",
    )(dest, zrow, h2)


def _experts_body(blk_ref, nblk_ref, xs_ref, wg_ref, wu_ref, wd_ref, y_ref):
    used = pl.program_id(0) < nblk_ref[0]

    @pl.when(used)
    def _():
        x = xs_ref[...].astype(BF16)
        hid = _silu(jnp.dot(x, wg_ref[0], preferred_element_type=F32)) * jnp.dot(x, wu_ref[0], preferred_element_type=F32)
        y_ref[...] = jnp.dot(hid.astype(BF16), wd_ref[0], preferred_element_type=F32)

    @pl.when(jnp.logical_not(used))
    def _():
        y_ref[...] = jnp.zeros_like(y_ref)


def _experts(blk_e, nblk, xs, wg, wu, wd):
    n_rows, d = xs.shape
    nb = n_rows // MOE_BLOCK

    def live(b, blk, nbl):
        return jnp.minimum(b, nbl[0] - 1)

    rows = lambda b, blk, nbl: (live(b, blk, nbl), 0)
    wsel = lambda b, blk, nbl: (blk[live(b, blk, nbl)], 0, 0)
    return pl.pallas_call(
        _experts_body,
        grid_spec=pltpu.PrefetchScalarGridSpec(
            num_scalar_prefetch=2,
            grid=(nb,),
            in_specs=[pl.BlockSpec((MOE_BLOCK, d), rows),
                      pl.BlockSpec((1, d, D_EXPERT), wsel),
                      pl.BlockSpec((1, d, D_EXPERT), wsel),
                      pl.BlockSpec((1, D_EXPERT, d), wsel)],
            out_specs=pl.BlockSpec((MOE_BLOCK, d), lambda b, blk, nbl: (b, 0))),
        out_shape=jax.ShapeDtypeStruct((n_rows, d), F32),
        compiler_params=_cparams("arbitrary"),
        name="moe_experts",
    )(blk_e, nblk, xs, wg, wu, wd)


def _combine_body(dest_ref, x_ref, wt_ref, ga_ref, fw_ref, yb_ref, o_ref, r0_ref, r1_ref, sem, *, final_norm):
    t = dest_ref.shape[1]

    def issue(jo, carry):
        for k in range(DMA_UNROLL):
            j = jo * DMA_UNROLL + k
            _row_copy(yb_ref, dest_ref[0, j], r0_ref, j, sem.at[0]).start()
            _row_copy(yb_ref, dest_ref[1, j], r1_ref, j, sem.at[1]).start()
        return carry

    lax.fori_loop(0, t // DMA_UNROLL, issue, 0)
    pltpu.make_async_copy(yb_ref.at[pl.ds(0, t), :], r0_ref, sem.at[0]).wait()
    pltpu.make_async_copy(yb_ref.at[pl.ds(0, t), :], r1_ref, sem.at[1]).wait()
    wt = wt_ref[...]
    moe = r0_ref[...] * wt[:, 0:1] + r1_ref[...] * wt[:, 1:2]
    x = x_ref[...] + _seq_rows(ga_ref, t) * moe
    if final_norm:
        x = x * lax.rsqrt(jnp.mean(x * x, axis=-1, keepdims=True) + EPS) * fw_ref[...]
    o_ref[...] = x


def _combine(dest, x2, wts_t, ga2, fw, yb, seq_len, final_norm):
    n, d = x2.shape
    t = min(DMA_TILE, n)
    row = lambda i: (i, 0)
    return pl.pallas_call(
        functools.partial(_combine_body, final_norm=final_norm),
        grid=(n // t,),
        in_specs=[pl.BlockSpec((2, t), lambda i: (0, i), memory_space=pltpu.SMEM),
                  pl.BlockSpec((t, d), row),
                  pl.BlockSpec((t, LANES), row),
                  _seq_block(t, seq_len, d),
                  pl.BlockSpec((1, d), lambda i: (0, 0)),
                  pl.BlockSpec(memory_space=pl.ANY)],
        out_specs=pl.BlockSpec((t, d), row),
        out_shape=jax.ShapeDtypeStruct((n, d), F32),
        scratch_shapes=[pltpu.VMEM((t, d), F32), pltpu.VMEM((t, d), F32), pltpu.SemaphoreType.DMA((2,))],
        compiler_params=_cparams("arbitrary"),
        name="moe_combine",
    )(dest, x2, wts_t, ga2, fw, yb)


def _lane_vec(vals, lane):
    return jnp.zeros((1, LANES), F32).at[0, lane:lane + vals.shape[0]].set(vals.astype(F32))


IN_SIZES = (SSD_CONV_DIM, SSD_D, SSD_HEADS, GMLP_D, GMLP_D, GDN_CONV_DIM, GDN_HEADS, GDN_HEADS, GDN_V_D)
IN_OFFSETS = tuple(sum(IN_SIZES[:i]) for i in range(len(IN_SIZES)))
WIDE_SEGMENTS = (0, 1, 3, 4, 5, 8)
GATE_COLUMNS = ((2, SSD_DT_LANE), (6, GDN_A_LANE), (7, GDN_B_LANE))


def _w_in_layout_body(w_ref, small_ref, o_ref):
    dst = 0
    for i in WIDE_SEGMENTS:
        o_ref[0, :, dst:dst + IN_SIZES[i]] = w_ref[0, :, IN_OFFSETS[i]:IN_OFFSETS[i] + IN_SIZES[i]].astype(BF16)
        dst += IN_SIZES[i]
    o_ref[0, :, dst:dst + LANES] = small_ref[0]


def _w_in_layout(w_in):
    depth, d, in_d = w_in.shape
    small = jnp.zeros((depth, d, LANES), BF16)
    for i, lane in GATE_COLUMNS:
        small = small.at[:, :, lane:lane + IN_SIZES[i]].set(w_in[:, :, IN_OFFSETS[i]:IN_OFFSETS[i] + IN_SIZES[i]].astype(BF16))
    tr = 256
    return pl.pallas_call(
        _w_in_layout_body,
        grid=(depth, d // tr),
        in_specs=[pl.BlockSpec((1, tr, in_d), lambda l, i: (l, i, 0)),
                  pl.BlockSpec((1, tr, LANES), lambda l, i: (l, i, 0))],
        out_specs=pl.BlockSpec((1, tr, IN_R), lambda l, i: (l, i, 0)),
        out_shape=jax.ShapeDtypeStruct((depth, d, IN_R), BF16),
        compiler_params=_cparams("parallel", "parallel"),
        name="w_in_layout",
    )(w_in, small)


def _layer_params(p, l, w_r):
    w_out = p["w_out"][l].astype(BF16)
    return dict(
        w_r=w_r[l],
        wa=w_out[:SSD_D], wb=w_out[SSD_D:SSD_D + GMLP_D], wc=w_out[SSD_D + GMLP_D:],
        norm_mix=p["norm_mix"][l][None, :], norm_ffn=p["norm_ffn"][l][None, :],
        ssd_cw=p["ssd_conv_w"][l], ssd_cb=p["ssd_conv_b"][l][None, :],
        ssd_dtb=_lane_vec(p["ssd_dt_bias"][l], SSD_DT_LANE), ssd_alog=_lane_vec(p["ssd_a_log"][l], SSD_DT_LANE),
        ssd_dvec=jnp.repeat(p["ssd_d"][l].astype(F32), SSD_HEAD_DIM)[None, :], ssd_nw=p["ssd_norm_w"][l][None, :],
        lng=p["gmlp_ln_g"][l][None, :], lnb=p["gmlp_ln_b"][l][None, :], ws=p["gmlp_ws"][l], bs=p["gmlp_bs"][l],
        gdn_cw=p["gdn_conv_w"][l],
        gdn_dtb=_lane_vec(p["gdn_dt_bias"][l], GDN_A_LANE), gdn_alog=_lane_vec(p["gdn_a_log"][l], GDN_A_LANE),
        gdn_nw=p["gdn_norm_w"][l][None, :],
        wg=p["w_gate"][l].astype(BF16), wu=p["w_up"][l].astype(BF16), wd=p["w_down"][l].astype(BF16),
    )


def _trunk(x, mod, ssd_h0, ssd_buf0, gdn_s0, gdn_buf0, p, layers):
    bsz, seq_len, d = x.shape
    n = bsz * seq_len
    x2 = x.reshape(n, d)
    n_rows = -(-(2 * n) // MOE_BLOCK) * MOE_BLOCK + N_EXPERTS * MOE_BLOCK
    qg = min(GMLP_CHUNK, seq_len)
    wrt = p["w_router"].T.astype(BF16)
    br = p["b_router"].astype(F32)[:, None]
    fw = p["norm_final"][None, :]
    outs = ([], [], [], [], [])
    depth = len(layers)
    for l, lp in enumerate(layers):
        sh1, sc1, ga1, sh2, sc2, ga2 = (m[:, None, :] for m in jnp.split(mod[l], 6, axis=-1))
        xbc, z, u, v, qkv, gg, sm = _inproj(x2, sh1, sc1, lp["norm_mix"], lp["w_r"], seq_len)
        ya, ssd_ht, ssd_buf = _ssd(xbc, z, sm, jnp.swapaxes(ssd_h0[l], -1, -2), ssd_buf0[l], lp["ssd_cw"], lp["ssd_cb"],
                                   lp["ssd_dtb"], lp["ssd_alog"], lp["ssd_dvec"], lp["ssd_nw"], bsz, seq_len)
        yb, vn = _gmlp(u, v, lp["lng"], lp["lnb"], lp["ws"][:, :qg, :qg], lp["bs"][:, :qg].T, bsz, seq_len)
        yc, gdn_s, gdn_buf = _gdn(qkv, gg, sm, gdn_s0[l], gdn_buf0[l], lp["gdn_cw"], lp["gdn_dtb"], lp["gdn_alog"],
                                  lp["gdn_nw"], bsz, seq_len)
        x2, h2, eidx, wts_t = _outproj_route(x2, ya, yb, yc, lp["wa"], lp["wb"], lp["wc"], ga1, lp["norm_ffn"],
                                             sc2, sh2, wrt, br, seq_len)
        dest, blk_e, nblk, zrow = _positions(eidx, n_rows // MOE_BLOCK)
        xs = _dispatch(dest, zrow, h2, n_rows)
        ye = _experts(blk_e[0], nblk[0, :1], xs, lp["wg"], lp["wu"], lp["wd"])
        x2 = _combine(dest, x2, wts_t, ga2, fw, ye, seq_len, final_norm=(l == depth - 1))
        for lst, s in zip(outs, (jnp.swapaxes(ssd_ht, -1, -2), ssd_buf, gdn_s, gdn_buf, vn.reshape(bsz, seq_len, GMLP_D))):
            lst.append(s)
    return (x2.reshape(bsz, seq_len, d),) + tuple(jnp.stack(o) for o in outs)


def kernel(x_prompt, x_sample, state_ssd, state_ssd_conv, state_gdn, state_gdn_conv, c_prompt, c_sample,
           w_ada, b_ada, norm_mix, norm_ffn, norm_final, w_in, w_out,
           ssd_conv_w, ssd_conv_b, ssd_dt_bias, ssd_a_log, ssd_d, ssd_norm_w,
           gmlp_ln_g, gmlp_ln_b, gmlp_ws, gmlp_bs,
           gdn_conv_w, gdn_a_log, gdn_dt_bias, gdn_norm_w,
           w_router, b_router, w_gate, w_up, w_down):
    p = dict(norm_mix=norm_mix, norm_ffn=norm_ffn, norm_final=norm_final, w_in=w_in, w_out=w_out,
             ssd_conv_w=ssd_conv_w, ssd_conv_b=ssd_conv_b, ssd_dt_bias=ssd_dt_bias, ssd_a_log=ssd_a_log,
             ssd_d=ssd_d, ssd_norm_w=ssd_norm_w, gmlp_ln_g=gmlp_ln_g, gmlp_ln_b=gmlp_ln_b, gmlp_ws=gmlp_ws,
             gmlp_bs=gmlp_bs, gdn_conv_w=gdn_conv_w, gdn_a_log=gdn_a_log, gdn_dt_bias=gdn_dt_bias,
             gdn_norm_w=gdn_norm_w, w_router=w_router, b_router=b_router, w_gate=w_gate, w_up=w_up, w_down=w_down)
    depth = w_in.shape[0]
    w_r = _w_in_layout(w_in)
    layers = [_layer_params(p, l, w_r) for l in range(depth)]
    bp = x_prompt.shape[0]
    mod = _ada_mod(jnp.concatenate([c_prompt, c_sample], axis=0), w_ada.astype(BF16), b_ada)
    z_ssd = jnp.zeros((depth, bp) + state_ssd.shape[2:], state_ssd.dtype)
    z_ssd_conv = jnp.zeros((depth, bp) + state_ssd_conv.shape[2:], x_prompt.dtype)
    z_gdn = jnp.zeros((depth, bp) + state_gdn.shape[2:], state_gdn.dtype)
    z_gdn_conv = jnp.zeros((depth, bp) + state_gdn_conv.shape[2:], x_prompt.dtype)
    y_p, p_ssd, p_ssd_conv, p_gdn, p_gdn_conv, _ = _trunk(
        x_prompt, mod[:, :bp], z_ssd, z_ssd_conv, z_gdn, z_gdn_conv, p, layers)
    y_s, s_ssd, s_ssd_conv, s_gdn, s_gdn_conv, s_gmlp_v = _trunk(
        x_sample, mod[:, bp:], state_ssd, state_ssd_conv, state_gdn, state_gdn_conv, p, layers)
    return (y_p, y_s, p_ssd, p_ssd_conv, p_gdn, p_gdn_conv, s_ssd, s_ssd_conv, s_gdn, s_gdn_conv, s_gmlp_v)
```

```python
import functools

import jax
import jax.numpy as jnp
from jax import lax
from jax.experimental import pallas as pl
from jax.experimental.pallas import tpu as pltpu

F32 = jnp.float32
BF16 = jnp.bfloat16
I32 = jnp.int32

LANES = 128
SUBLANES = 8
VMEM_LIMIT_BYTES = 56 * 1024 * 1024

D_MODEL = 1024
EPS = 1e-6
CHUNK = 64
CONV_W = 4
SSD_HEADS = 6
SSD_HEAD_DIM = 64
SSD_STATE = 64
SSD_GROUPS = 2
SSD_D = SSD_HEADS * SSD_HEAD_DIM
SSD_CONV_DIM = SSD_D + 2 * SSD_GROUPS * SSD_STATE
GMLP_GROUPS = 4
GMLP_GROUP_DIM = 64
GMLP_D = GMLP_GROUPS * GMLP_GROUP_DIM
GMLP_CHUNK = 128
GDN_HEADS = 6
GDN_KEY_DIM = 64
GDN_VAL_DIM = 64
GDN_QK_D = GDN_HEADS * GDN_KEY_DIM
GDN_V_D = GDN_HEADS * GDN_VAL_DIM
GDN_CONV_DIM = 2 * GDN_QK_D + GDN_V_D
GDN_PAIRS = GDN_HEADS // 2
SSD_PAIRS = SSD_HEADS // 2
assert 2 * SSD_HEAD_DIM == 2 * SSD_STATE == 2 * GDN_KEY_DIM == 2 * GDN_VAL_DIM == 2 * GMLP_GROUP_DIM == 128
assert SSD_PAIRS == GDN_PAIRS and GMLP_GROUPS % 2 == 0
N_EXPERTS = 16
N_EXPERT_GROUPS = 4
EXPERTS_PER_GROUP = N_EXPERTS // N_EXPERT_GROUPS
D_EXPERT = 512
PAIRS_PER_GROUP = EXPERTS_PER_GROUP * (EXPERTS_PER_GROUP - 1) // 2
N_CLASSES = N_EXPERT_GROUPS * PAIRS_PER_GROUP
CLASS_ROWS = -(-N_CLASSES // SUBLANES) * SUBLANES
assert EXPERTS_PER_GROUP == 4 and 2 * CLASS_ROWS <= LANES

IN_SIZES = (SSD_CONV_DIM, SSD_D, SSD_HEADS, GMLP_D, GMLP_D, GDN_CONV_DIM, GDN_HEADS, GDN_HEADS, GDN_V_D)
IN_OFFSETS = tuple(sum(IN_SIZES[:i]) for i in range(len(IN_SIZES)))
WIDE_SEGMENTS = (("qkv", 5), ("z", 1), ("gate", 8), ("xbc", 0), ("u", 3), ("v", 4))
SSD_DT_LANE = 0
GDN_A_LANE = 8
GDN_B_LANE = 16
GATE_COLUMNS = ((2, SSD_DT_LANE), (6, GDN_A_LANE), (7, GDN_B_LANE))
COLS = {}
for _name, _i in WIDE_SEGMENTS:
    COLS[_name] = (sum(w for _, w in COLS.values()), IN_SIZES[_i])
COLS["small"] = (sum(w for _, w in COLS.values()), LANES)
IN_R = sum(w for _, w in COLS.values())
assert all(off % w == 0 for off, w in COLS.values())
TAIL_ROW = SUBLANES - (CONV_W - 1)

MOE_BLOCK = 256
ROW_TILE = 512
POS_TILE = 2048
GDN_STEP_CHUNKS = 2
SSD_STEP_CHUNKS = 2
GMLP_STEP_CHUNKS = 2
DMA_TILE = 512
DMA_UNROLL = 8


def _cparams(*sem):
    return pltpu.CompilerParams(dimension_semantics=sem, vmem_limit_bytes=VMEM_LIMIT_BYTES)


def _bdot(a, b):
    return jnp.dot(a.astype(BF16), b.astype(BF16), preferred_element_type=F32)


def _bdot_nt(a, b):
    return lax.dot_general(a.astype(BF16), b.astype(BF16), (((1,), (1,)), ((), ())), preferred_element_type=F32)


def _bdot_tn(a, b):
    return lax.dot_general(a.astype(BF16), b.astype(BF16), (((0,), (0,)), ((), ())), preferred_element_type=F32)


def _tn_split(a, b):
    dims = (((0,), (0,)), ((), ()))
    a_hi, b_hi = a.astype(BF16), b.astype(BF16)
    a_lo = (a - a_hi.astype(F32)).astype(BF16)
    b_lo = (b - b_hi.astype(F32)).astype(BF16)
    dot = lambda x, y: lax.dot_general(x, y, dims, preferred_element_type=F32)
    return dot(a_hi, b_hi) + (dot(a_lo, b_hi) + dot(a_hi, b_lo))


def _sigmoid(x):
    return 1.0 / (1.0 + jnp.exp(-x))


def _silu(x):
    return x * _sigmoid(x)


def _softplus(x):
    return jnp.maximum(x, 0.0) + jnp.log1p(jnp.exp(-jnp.abs(x)))


def _gelu_tanh(x):
    return 0.5 * x * (1.0 + jnp.tanh(0.7978845608028654 * (x + 0.044715 * (x * x * x))))


def _cumsum_rows(a):
    n = a.shape[0]
    row = lax.broadcasted_iota(I32, a.shape, 0)
    s = 1
    while s < n:
        a = a + jnp.where(row >= s, pltpu.roll(a, s, axis=0), jnp.zeros_like(a))
        s *= 2
    return a


def _cumsum_lanes(a):
    n = a.shape[1]
    col = lax.broadcasted_iota(I32, a.shape, 1)
    s = 1
    while s < n:
        a = a + jnp.where(col >= s, pltpu.roll(a, s, axis=1), jnp.zeros_like(a))
        s *= 2
    return a


def _rows_to_lanes(a):
    q = a.shape[0]
    if q < LANES:
        a = jnp.concatenate([a, jnp.zeros((LANES - q, LANES), a.dtype)], axis=0)
    return a.T[:, :q]


def _ada_body(c_ref, w_ref, b_ref, o_ref):
    o_ref[0] = _bdot(_silu(c_ref[...]), w_ref[0]) + b_ref[0]


def _ada_mod(c_all, w_ada, b_ada):
    depth, d, n6 = w_ada.shape
    bc = c_all.shape[0]
    tn = 1536
    return pl.pallas_call(
        _ada_body,
        grid=(depth, n6 // tn),
        in_specs=[pl.BlockSpec((bc, d), lambda l, j: (0, 0)),
                  pl.BlockSpec((1, d, tn), lambda l, j: (l, 0, j)),
                  pl.BlockSpec((1, 1, tn), lambda l, j: (l, 0, j))],
        out_specs=pl.BlockSpec((1, bc, tn), lambda l, j: (l, 0, j)),
        out_shape=jax.ShapeDtypeStruct((depth, bc, n6), F32),
        compiler_params=_cparams("parallel", "parallel"),
        name="ada_mod",
    )(c_all, w_ada, b_ada.reshape(depth, 1, n6))


def _seq_block(tm, seq_len, d):
    if tm <= seq_len:
        per_seq = seq_len // tm
        return pl.BlockSpec((1, 1, d), lambda i: (i // per_seq, 0, 0))
    return pl.BlockSpec((tm // seq_len, 1, d), lambda i: (i, 0, 0))


def _seq_rows(ref, tm):
    s = ref.shape[0]
    if s == 1:
        return ref[0]
    return jnp.concatenate([jnp.broadcast_to(ref[k], (tm // s, ref.shape[2])) for k in range(s)], axis=0)


def _mod_rms(x, nw, sc, sh):
    ms = jnp.mean(x * x, axis=-1, keepdims=True)
    return (x * lax.rsqrt(ms + EPS) * nw) * (1.0 + sc) + sh


def _inproj_body(x_ref, sh_ref, sc_ref, nw_ref, w_ref, o_ref):
    tm = x_ref.shape[0]
    hb = _mod_rms(x_ref[...], nw_ref[...], _seq_rows(sc_ref, tm), _seq_rows(sh_ref, tm)).astype(BF16)
    o_ref[...] = jnp.dot(hb, w_ref[...], preferred_element_type=F32)


def _inproj(x2, sh, sc, nw, w_r, seq_len):
    n, d = x2.shape
    tm = min(ROW_TILE, n)
    row = lambda i: (i, 0)
    return pl.pallas_call(
        _inproj_body,
        grid=(n // tm,),
        in_specs=[pl.BlockSpec((tm, d), row),
                  _seq_block(tm, seq_len, d),
                  _seq_block(tm, seq_len, d),
                  pl.BlockSpec((1, d), lambda i: (0, 0)),
                  pl.BlockSpec((d, IN_R), lambda i: (0, 0))],
        out_specs=pl.BlockSpec((tm, IN_R), row),
        out_shape=jax.ShapeDtypeStruct((n, IN_R), F32),
        compiler_params=_cparams("parallel"),
        name="inproj",
    )(x2, sh, sc, nw, w_r)


def _proj_cols(name, rows, row_index):
    off, width = COLS[name]
    return pl.BlockSpec((rows, width), lambda *g: (row_index(*g), off // width))


def _causal_conv_step(x_ref, buf0_ref, bufout_ref, xp_ref, cw_ref, first):
    q = x_ref.shape[0]

    @pl.when(first)
    def _():
        xp_ref[TAIL_ROW:SUBLANES, :] = buf0_ref[0]

    xp_ref[SUBLANES:SUBLANES + q, :] = x_ref[...]
    acc = cw_ref[0:1, :] * xp_ref[TAIL_ROW:TAIL_ROW + q, :]
    for k in range(1, CONV_W):
        acc = acc + cw_ref[k:k + 1, :] * xp_ref[TAIL_ROW + k:TAIL_ROW + k + q, :]
    tail = xp_ref[TAIL_ROW + q:SUBLANES + q, :]
    bufout_ref[0] = tail
    xp_ref[TAIL_ROW:SUBLANES, :] = tail
    return acc


def _decay_terms(g):
    q = g.shape[0]
    cum = _cumsum_rows(g)
    cum_t = _rows_to_lanes(cum)
    last = cum[q - 1:q, :]
    return cum, cum_t, jnp.exp(cum), jnp.exp(last - cum), jnp.exp(last)


def _ssd_body(xbc_ref, z_ref, sm_ref, h0_ref, buf0_ref, cw_ref, cb_ref, dtb_ref, alog_ref, dvec_ref, nw_ref,
              y_ref, hout_ref, bufout_ref, xp_ref, ht_ref):
    t = xbc_ref.shape[0]
    q = min(CHUNK, t)
    first = pl.program_id(1) == 0

    @pl.when(first)
    def _():
        ht_ref[...] = h0_ref[0]

    xa = _silu(_causal_conv_step(xbc_ref, buf0_ref, bufout_ref, xp_ref, cw_ref, first) + cb_ref[...])
    dt = _softplus(sm_ref[...] + dtb_ref[...])
    a_all = dt * (-jnp.exp(alog_ref[...]))
    causal = lax.broadcasted_iota(I32, (q, LANES), 0) >= (lax.broadcasted_iota(I32, (q, LANES), 1) & (q - 1))
    lo = _low_half((q, LANES))
    b_off, c_off = SSD_D, SSD_D + SSD_GROUPS * SSD_STATE
    y_loc, st, ce, ed = {}, {}, {}, {}
    for c in range(t // q):
        rows = slice(c * q, (c + 1) * q)
        cum = _cumsum_rows(a_all[rows])
        cum_rows = jnp.concatenate([cum, pltpu.roll(cum, LANES - 1, axis=1)], axis=0).T
        last = cum[q - 1:q, :]
        ecum, wend, elast = jnp.exp(cum), jnp.exp(last - cum), jnp.exp(last)
        bm = xa[rows, b_off:b_off + LANES]
        cm = xa[rows, c_off:c_off + LANES]
        bm_sw = pltpu.roll(bm, SSD_STATE, axis=1)
        cm_sw = pltpu.roll(cm, SSD_STATE, axis=1)
        b_pairs = [jnp.where(lo, bm, bm_sw), bm, jnp.where(lo, bm_sw, bm)]
        c_pairs = [jnp.where(lo, cm, cm_sw), cm, jnp.where(lo, cm_sw, cm)]
        for p in range(SSD_PAIRS):
            lane = SSD_DT_LANE + 2 * p
            ls = slice(p * LANES, (p + 1) * LANES)
            xh = xa[rows, ls]
            xdt = xh * _pair_cols(dt[rows], lane)
            e = jnp.exp(_pair_cols(cum, lane) - cum_rows[lane:lane + 1, :])
            cb = lax.dot_general(c_pairs[p].astype(BF16), _block_diag(b_pairs[p].astype(BF16)),
                                 (((1,), (1,)), ((), ())), preferred_element_type=F32)
            y_loc[c, p] = _pdot(cb * jnp.where(causal, e, 0.0), xdt) + dvec_ref[:, ls] * xh
            st[c, p] = _tn_split(b_pairs[p] * _pair_cols(wend, lane), xdt)
            ce[c, p] = c_pairs[p] * _pair_cols(ecum, lane)
            ed[c, p] = _pair_cols(elast, lane)
    h_cur = [ht_ref[p] for p in range(SSD_PAIRS)]
    half = SSD_D // SSD_GROUPS
    for c in range(t // q):
        rows = slice(c * q, (c + 1) * q)
        yz = []
        for p in range(SSD_PAIRS):
            ls = slice(p * LANES, (p + 1) * LANES)
            y = y_loc[c, p] + _pdot(ce[c, p], h_cur[p])
            h_cur[p] = h_cur[p] * ed[c, p] + jnp.where(lo, st[c, p][:q], st[c, p][q:])
            yz.append(y * _silu(z_ref[rows, ls]))
        sq = [v * v for v in yz]
        row_sum = lambda v: jnp.sum(v, axis=-1, keepdims=True)
        ss0 = row_sum(sq[0]) + row_sum(jnp.where(lo, sq[1], 0.0))
        ss1 = row_sum(jnp.where(lo, 0.0, sq[1])) + row_sum(sq[2])
        r0 = lax.rsqrt(ss0 * (1.0 / half) + EPS)
        r1 = lax.rsqrt(ss1 * (1.0 / half) + EPS)
        for p, scale in enumerate((r0, jnp.where(lo, r0, r1), r1)):
            ls = slice(p * LANES, (p + 1) * LANES)
            y_ref[rows, ls] = yz[p] * scale * nw_ref[:, ls]
    for p in range(SSD_PAIRS):
        ht_ref[p] = h_cur[p]
        hout_ref[0, p] = h_cur[p]


def _ssd(proj, h0t, buf0, cw, cb, dtb, alog, dvec, nw, bsz, seq_len):
    t = min(CHUNK * SSD_STEP_CHUNKS, seq_len)
    nc = seq_len // t
    n = bsz * seq_len
    tok = lambda b, c: (b * nc + c, 0)
    tok_row = lambda b, c: b * nc + c
    per_b4 = lambda b, c: (b, 0, 0, 0)
    per_b3 = lambda b, c: (b, 0, 0)
    const = lambda b, c: (0, 0)
    state = (SSD_PAIRS, SSD_STATE, 2 * SSD_HEAD_DIM)
    y, h_pairs, buf = pl.pallas_call(
        _ssd_body,
        grid=(bsz, nc),
        in_specs=[_proj_cols("xbc", t, tok_row),
                  _proj_cols("z", t, tok_row),
                  _proj_cols("small", t, tok_row),
                  pl.BlockSpec((1,) + state, per_b4),
                  pl.BlockSpec((1, CONV_W - 1, SSD_CONV_DIM), per_b3),
                  pl.BlockSpec((CONV_W, SSD_CONV_DIM), const),
                  pl.BlockSpec((1, SSD_CONV_DIM), const),
                  pl.BlockSpec((1, LANES), const),
                  pl.BlockSpec((1, LANES), const),
                  pl.BlockSpec((1, SSD_D), const),
                  pl.BlockSpec((1, SSD_D), const)],
        out_specs=[pl.BlockSpec((t, SSD_D), tok),
                   pl.BlockSpec((1,) + state, per_b4),
                   pl.BlockSpec((1, CONV_W - 1, SSD_CONV_DIM), per_b3)],
        out_shape=[jax.ShapeDtypeStruct((n, SSD_D), F32),
                   jax.ShapeDtypeStruct((bsz,) + state, F32),
                   jax.ShapeDtypeStruct((bsz, CONV_W - 1, SSD_CONV_DIM), F32)],
        scratch_shapes=[pltpu.VMEM((SUBLANES + t, SSD_CONV_DIM), F32),
                        pltpu.VMEM(state, F32)],
        compiler_params=_cparams("parallel", "arbitrary"),
        name="ssd",
    )(proj, proj, proj, _heads_to_pairs(h0t), buf0, cw, cb, dtb, alog, dvec, nw)
    return y, _pairs_to_heads(h_pairs), buf


def _gmlp_body(u_ref, v_ref, lng_ref, lnb_ref, ws_ref, bst_ref, y_ref, vn_ref):
    t = u_ref.shape[0]
    q = ws_ref.shape[1]
    lower = lax.broadcasted_iota(I32, (q, q), 0) >= lax.broadcasted_iota(I32, (q, q), 1)
    w_low = [jnp.where(lower, ws_ref[g], 0.0).astype(BF16) for g in range(GMLP_GROUPS)]
    lo = _low_half((q, LANES))
    inv_dim = 1.0 / GMLP_GROUP_DIM
    work = [(c, p) for c in range(t // q) for p in range(GMLP_GROUPS // 2)]
    vnb = {}
    for c, p in work:
        rows, ls = slice(c * q, (c + 1) * q), slice(p * LANES, (p + 1) * LANES)
        gv = _gelu_tanh(v_ref[rows, ls])
        dv = gv - _half_sums(gv) * inv_dim
        vn = dv * lax.rsqrt(_half_sums(dv * dv) * inv_dim + EPS) * lng_ref[:, ls] + lnb_ref[:, ls]
        vn_ref[rows, ls] = vn
        vnb[c, p] = vn.astype(BF16)
    zero = jnp.zeros((q, LANES), BF16)
    mixed = {(c, p): jnp.dot(w_low[2 * p], jnp.where(lo, vnb[c, p], zero), preferred_element_type=F32)
             + jnp.dot(w_low[2 * p + 1], jnp.where(lo, zero, vnb[c, p]), preferred_element_type=F32) for c, p in work}
    for c, p in work:
        rows, ls = slice(c * q, (c + 1) * q), slice(p * LANES, (p + 1) * LANES)
        y_ref[rows, ls] = _gelu_tanh(u_ref[rows, ls]) * (mixed[c, p] + _pair_cols(bst_ref[...], 2 * p))


def _gmlp(proj, lng, lnb, ws, bst, bsz, seq_len):
    q = ws.shape[1]
    t = min(q * GMLP_STEP_CHUNKS, seq_len)
    n = bsz * seq_len
    tok = lambda i: (i, 0)
    return pl.pallas_call(
        _gmlp_body,
        grid=(n // t,),
        in_specs=[_proj_cols("u", t, lambda i: i),
                  _proj_cols("v", t, lambda i: i),
                  pl.BlockSpec((1, GMLP_D), lambda i: (0, 0)),
                  pl.BlockSpec((1, GMLP_D), lambda i: (0, 0)),
                  pl.BlockSpec((GMLP_GROUPS, q, q), lambda i: (0, 0, 0)),
                  pl.BlockSpec((q, GMLP_GROUPS), lambda i: (0, 0))],
        out_specs=[pl.BlockSpec((t, GMLP_D), tok), pl.BlockSpec((t, GMLP_D), tok)],
        out_shape=[jax.ShapeDtypeStruct((n, GMLP_D), F32), jax.ShapeDtypeStruct((n, GMLP_D), F32)],
        compiler_params=_cparams("parallel"),
        name="gmlp",
    )(proj, proj, lng, lnb, ws, bst)


def _low_half(shape):
    return lax.broadcasted_iota(I32, shape, 1) < GDN_KEY_DIM


def _pair_cols(m, lane):
    shape = (m.shape[0], LANES)
    return jnp.where(_low_half(shape), m[:, lane:lane + 1], m[:, lane + 1:lane + 2])


def _half_sums(s):
    lo = _low_half(s.shape)
    s_lo = jnp.sum(jnp.where(lo, s, 0.0), axis=-1, keepdims=True)
    s_hi = jnp.sum(jnp.where(lo, 0.0, s), axis=-1, keepdims=True)
    return jnp.where(lo, s_lo, s_hi)


def _l2_halves(x):
    return x * lax.rsqrt(_half_sums(x * x) + EPS)


def _block_diag(y):
    q = y.shape[0]
    yy = jnp.concatenate([y, y], axis=0)
    same = (lax.broadcasted_iota(I32, yy.shape, 0) < q) == _low_half(yy.shape)
    return jnp.where(same, yy, jnp.zeros_like(yy))


def _pdot(x, y):
    return jnp.dot(x.astype(BF16), _block_diag(y.astype(BF16)), preferred_element_type=F32)


def _unit_lower_inverse_minus_identity(mats):
    q = mats[0].shape[0]
    ii = lax.broadcasted_iota(I32, (q, LANES), 0)
    jj = lax.broadcasted_iota(I32, (q, LANES), 1) & (q - 1)
    base = (ii >> 1) == (jj >> 1)
    rs = [-jnp.where(base, a, 0.0) for a in mats]
    level = 1
    while (1 << level) < q:
        in_pair = (ii >> (level + 1)) == (jj >> (level + 1))
        off_diag = (ii >> level) != (jj >> level)
        mask = jnp.logical_and(in_pair, off_diag)
        ams = [jnp.where(mask, a, 0.0) for a in mats]
        ws = [am + _pdot(am, r) for am, r in zip(ams, rs)]
        rs = [r - w - _pdot(r, w) for r, w in zip(rs, ws)]
        level += 1
    return rs


def _gdn_body(qkv_ref, gg_ref, sm_ref, s0_ref, buf0_ref, cw_ref, dtb_ref, alog_ref, nw_ref,
              y_ref, sout_ref, bufout_ref, xp_ref, s_ref):
    t = qkv_ref.shape[0]
    q = min(CHUNK, t)
    first = pl.program_id(1) == 0

    @pl.when(first)
    def _():
        s_ref[...] = s0_ref[0]

    xa = _silu(_causal_conv_step(qkv_ref, buf0_ref, bufout_ref, xp_ref, cw_ref, first))
    sm = sm_ref[...]
    beta_all = _sigmoid(sm)
    g_all = -jnp.exp(alog_ref[...]) * _softplus(sm + dtb_ref[...])
    ii = lax.broadcasted_iota(I32, (q, LANES), 0)
    jj = lax.broadcasted_iota(I32, (q, LANES), 1) & (q - 1)
    lo = _low_half((q, LANES))
    chains = [(c, p) for c in range(t // q) for p in range(GDN_PAIRS)]
    qp, kp, kdec, edec, kk, a_mat, attn, vb, kbe, qe = {}, {}, {}, {}, {}, {}, {}, {}, {}, {}
    for c in range(t // q):
        rows = slice(c * q, (c + 1) * q)
        cum = _cumsum_rows(g_all[rows])
        cum_rows = jnp.concatenate([cum, pltpu.roll(cum, LANES - 1, axis=1)], axis=0).T
        last = cum[q - 1:q, :]
        ecum, kdecw, elast = jnp.exp(cum), jnp.exp(last - cum), jnp.exp(last)
        for p in range(GDN_PAIRS):
            a_lane, b_lane = GDN_A_LANE + 2 * p, GDN_B_LANE + 2 * p
            ls = slice(p * LANES, (p + 1) * LANES)
            qp[c, p] = _l2_halves(xa[rows, ls]) * (GDN_KEY_DIM ** -0.5)
            kp[c, p] = _l2_halves(xa[rows, GDN_QK_D + p * LANES:GDN_QK_D + (p + 1) * LANES])
            beta = _pair_cols(beta_all[rows], b_lane)
            kb = kp[c, p] * beta
            e = jnp.exp(_pair_cols(cum, a_lane) - cum_rows[a_lane:a_lane + 1, :])
            ecp = _pair_cols(ecum, a_lane)
            kdec[c, p] = kp[c, p] * _pair_cols(kdecw, a_lane)
            edec[c, p] = _pair_cols(elast, a_lane)
            kk[c, p] = lax.dot_general(jnp.concatenate([kb, qp[c, p]], axis=0).astype(BF16),
                                       _block_diag(kp[c, p].astype(BF16)),
                                       (((1,), (1,)), ((), ())), preferred_element_type=F32)
            a_mat[c, p] = kk[c, p][:q] * jnp.where(ii > jj, e, 0.0)
            attn[c, p] = kk[c, p][q:] * jnp.where(ii >= jj, e, 0.0)
            vb[c, p] = xa[rows, 2 * GDN_QK_D + p * LANES:2 * GDN_QK_D + (p + 1) * LANES] * beta
            kbe[c, p] = kb * ecp
            qe[c, p] = qp[c, p] * ecp
    r = dict(zip(chains, _unit_lower_inverse_minus_identity([a_mat[cp] for cp in chains])))
    u_base = {cp: vb[cp] + _pdot(r[cp], vb[cp]) for cp in chains}
    k_cd = {cp: kbe[cp] + _pdot(r[cp], kbe[cp]) for cp in chains}
    s_cur = [s_ref[p] for p in range(GDN_PAIRS)]
    for c in range(t // q):
        rows = slice(c * q, (c + 1) * q)
        ks = [jnp.dot(jnp.concatenate([k_cd[c, p], qe[c, p]], axis=0).astype(BF16),
                      _block_diag(s_cur[p].astype(BF16)), preferred_element_type=F32) for p in range(GDN_PAIRS)]
        u = [u_base[c, p] - ks[p][:q] for p in range(GDN_PAIRS)]
        au = [_pdot(attn[c, p], u[p]) for p in range(GDN_PAIRS)]
        ku = [_bdot_tn(kdec[c, p], u[p]) for p in range(GDN_PAIRS)]
        for p in range(GDN_PAIRS):
            s_cur[p] = s_cur[p] * edec[c, p] + jnp.where(lo, ku[p][:q], ku[p][q:])
            o = ks[p][q:] + au[p]
            o = o * lax.rsqrt(_half_sums(o * o) * (1.0 / GDN_VAL_DIM) + EPS) * nw_ref[...]
            ls = slice(p * LANES, (p + 1) * LANES)
            y_ref[rows, ls] = o * _silu(gg_ref[rows, ls])
    for p in range(GDN_PAIRS):
        s_ref[p] = s_cur[p]
        sout_ref[0, p] = s_cur[p]


def _heads_to_pairs(s):
    b = s.shape[0]
    s = s.reshape(b, GDN_PAIRS, 2, GDN_KEY_DIM, GDN_VAL_DIM)
    return jnp.swapaxes(s, 2, 3).reshape(b, GDN_PAIRS, GDN_KEY_DIM, 2 * GDN_VAL_DIM)


def _pairs_to_heads(s):
    b = s.shape[0]
    s = s.reshape(b, GDN_PAIRS, GDN_KEY_DIM, 2, GDN_VAL_DIM)
    return jnp.swapaxes(s, 2, 3).reshape(b, GDN_HEADS, GDN_KEY_DIM, GDN_VAL_DIM)


def _gdn(proj, s0, buf0, cw, dtb, alog, nw, bsz, seq_len):
    t = min(CHUNK * GDN_STEP_CHUNKS, seq_len)
    nc = seq_len // t
    n = bsz * seq_len
    tok = lambda b, c: (b * nc + c, 0)
    tok_row = lambda b, c: b * nc + c
    per_b4 = lambda b, c: (b, 0, 0, 0)
    per_b3 = lambda b, c: (b, 0, 0)
    const = lambda b, c: (0, 0)
    state = (GDN_PAIRS, GDN_KEY_DIM, 2 * GDN_VAL_DIM)
    y, s_pairs, buf = pl.pallas_call(
        _gdn_body,
        grid=(bsz, nc),
        in_specs=[_proj_cols("qkv", t, tok_row),
                  _proj_cols("gate", t, tok_row),
                  _proj_cols("small", t, tok_row),
                  pl.BlockSpec((1,) + state, per_b4),
                  pl.BlockSpec((1, CONV_W - 1, GDN_CONV_DIM), per_b3),
                  pl.BlockSpec((CONV_W, GDN_CONV_DIM), const),
                  pl.BlockSpec((1, LANES), const),
                  pl.BlockSpec((1, LANES), const),
                  pl.BlockSpec((1, 2 * GDN_VAL_DIM), const)],
        out_specs=[pl.BlockSpec((t, GDN_V_D), tok),
                   pl.BlockSpec((1,) + state, per_b4),
                   pl.BlockSpec((1, CONV_W - 1, GDN_CONV_DIM), per_b3)],
        out_shape=[jax.ShapeDtypeStruct((n, GDN_V_D), F32),
                   jax.ShapeDtypeStruct((bsz,) + state, F32),
                   jax.ShapeDtypeStruct((bsz, CONV_W - 1, GDN_CONV_DIM), F32)],
        scratch_shapes=[pltpu.VMEM((SUBLANES + t, GDN_CONV_DIM), F32),
                        pltpu.VMEM(state, F32)],
        compiler_params=_cparams("parallel", "arbitrary"),
        name="gdn",
    )(proj, proj, proj, _heads_to_pairs(s0), buf0, cw, dtb, alog, jnp.concatenate([nw, nw], axis=1))
    return y, _pairs_to_heads(s_pairs), buf


def _route(logits, b_col):
    aff = _sigmoid(logits)
    sel = aff + b_col
    srow = [sel[e:e + 1, :] for e in range(N_EXPERTS)]
    arow = [aff[e:e + 1, :] for e in range(N_EXPERTS)]
    gscore = []
    for g in range(N_EXPERT_GROUPS):
        a, b, c, d = srow[g * EXPERTS_PER_GROUP:(g + 1) * EXPERTS_PER_GROUP]
        hi1, lo1, hi2, lo2 = jnp.maximum(a, b), jnp.minimum(a, b), jnp.maximum(c, d), jnp.minimum(c, d)
        gscore.append(jnp.maximum(hi1, hi2) + jnp.maximum(jnp.minimum(hi1, hi2), jnp.maximum(lo1, lo2)))
    best = jnp.zeros(gscore[0].shape, I32)
    best_v = gscore[0]
    for g in range(1, N_EXPERT_GROUPS):
        upd = gscore[g] > best_v
        best = jnp.where(upd, g, best)
        best_v = jnp.where(upd, gscore[g], best_v)
    cs, ca = [], []
    for j in range(EXPERTS_PER_GROUP):
        cj, aj = srow[j], arow[j]
        for g in range(1, N_EXPERT_GROUPS):
            m = best == g
            cj = jnp.where(m, srow[g * EXPERTS_PER_GROUP + j], cj)
            aj = jnp.where(m, arow[g * EXPERTS_PER_GROUP + j], aj)
        cs.append(cj)
        ca.append(aj)
    i1 = jnp.zeros(best.shape, I32)
    v1, a1 = cs[0], ca[0]
    for j in range(1, EXPERTS_PER_GROUP):
        upd = cs[j] > v1
        i1 = jnp.where(upd, j, i1)
        v1 = jnp.where(upd, cs[j], v1)
        a1 = jnp.where(upd, ca[j], a1)
    i2 = jnp.zeros(best.shape, I32)
    v2 = jnp.full(v1.shape, -jnp.inf, F32)
    a2 = jnp.zeros(v1.shape, F32)
    for j in range(EXPERTS_PER_GROUP):
        upd = jnp.logical_and(i1 != j, cs[j] > v2)
        i2 = jnp.where(upd, j, i2)
        v2 = jnp.where(upd, cs[j], v2)
        a2 = jnp.where(upd, ca[j], a2)
    tot = a1 + a2
    swap = i2 < i1
    lo, hi = jnp.where(swap, i2, i1), jnp.where(swap, i1, i2)
    pair = jnp.where(lo == 0, hi - 1, jnp.where(lo == 1, hi + 1, PAIRS_PER_GROUP - 1))
    w1, w2 = a1 / tot, a2 / tot
    return best * PAIRS_PER_GROUP + pair, jnp.where(swap, w2, w1), jnp.where(swap, w1, w2)


def _outproj_body(x_ref, ya_ref, yb_ref, yc_ref, wa_ref, wb_ref, wc_ref, ga_ref, nw_ref, sc_ref, sh_ref,
                  wrt_ref, br_ref, xo_ref, h2_ref, cls_ref):
    m = _bdot(ya_ref[...], wa_ref[...]) + _bdot(yb_ref[...], wb_ref[...]) + _bdot(yc_ref[...], wc_ref[...])
    tm, d = x_ref.shape
    x = x_ref[...] + _seq_rows(ga_ref, tm) * m
    xo_ref[...] = x
    h2 = _mod_rms(x, nw_ref[...], _seq_rows(sc_ref, tm), _seq_rows(sh_ref, tm))
    h2_ref[:, :d] = h2
    cls, w_lo, w_hi = _route(_bdot_nt(wrt_ref[...], h2), br_ref[...])
    cls_ref[...] = cls
    row = lax.broadcasted_iota(I32, (LANES, tm), 0)
    h2_ref[:, d:] = jnp.where(row == 0, w_lo, jnp.where(row == 1, w_hi, 0.0)).T


def _outproj_route(x2, ya, yb, yc, wa, wb, wc, ga1, nw, sc2, sh2, wrt, br, seq_len):
    n, d = x2.shape
    tm = min(ROW_TILE, n)
    row = lambda i: (i, 0)
    seq = _seq_block(tm, seq_len, d)
    const = lambda i: (0, 0)
    return pl.pallas_call(
        _outproj_body,
        grid=(n // tm,),
        in_specs=[pl.BlockSpec((tm, d), row),
                  pl.BlockSpec((tm, SSD_D), row),
                  pl.BlockSpec((tm, GMLP_D), row),
                  pl.BlockSpec((tm, GDN_V_D), row),
                  pl.BlockSpec((SSD_D, d), const),
                  pl.BlockSpec((GMLP_D, d), const),
                  pl.BlockSpec((GDN_V_D, d), const),
                  seq,
                  pl.BlockSpec((1, d), const),
                  seq,
                  seq,
                  pl.BlockSpec((N_EXPERTS, d), const),
                  pl.BlockSpec((N_EXPERTS, 1), const)],
        out_specs=[pl.BlockSpec((tm, d), row),
                   pl.BlockSpec((tm, d + LANES), row),
                   pl.BlockSpec((1, tm), lambda i: (0, i))],
        out_shape=[jax.ShapeDtypeStruct((n, d), F32),
                   jax.ShapeDtypeStruct((n, d + LANES), F32),
                   jax.ShapeDtypeStruct((1, n), I32)],
        compiler_params=_cparams("parallel"),
        name="outproj_route",
    )(x2, ya, yb, yc, wa, wb, wc, ga1, nw, sc2, sh2, wrt, br)


def _positions_body(cls_ref, dest_ref, blk_ref, nblk_ref, zrow_ref, cnt_ref, run_ref, *, nb_total):
    phase = pl.program_id(0)
    i = pl.program_id(1)
    t = cls_ref.shape[1]
    m0 = lax.broadcasted_iota(I32, (CLASS_ROWS, t), 0) == cls_ref[...]
    hits = jnp.where(m0, 1.0, 0.0)
    tile_cnt = jnp.broadcast_to(jnp.sum(hits, axis=1, keepdims=True), (CLASS_ROWS, LANES))

    @pl.when(jnp.logical_and(phase == 0, i == 0))
    def _():
        cnt_ref[...] = jnp.zeros_like(cnt_ref)

    @pl.when(phase == 0)
    def _():
        cnt_ref[...] += tile_cnt

    @pl.when(jnp.logical_and(phase == 1, i == 0))
    def _():
        padded = jnp.floor((cnt_ref[...] + (MOE_BLOCK - 1)) * (1.0 / MOE_BLOCK)) * MOE_BLOCK
        pad_end = _cumsum_rows(padded)
        run_ref[...] = pad_end - padded
        nb = blk_ref.shape[1]
        blk_start = (lax.broadcasted_iota(I32, (CLASS_ROWS, nb), 1) * MOE_BLOCK).astype(F32)
        past = jnp.where(blk_start >= pad_end[:, 0:1], 1.0, 0.0)
        cls = jnp.minimum(jnp.sum(past, axis=0, keepdims=True), N_CLASSES - 1.0)
        grp = jnp.floor(cls * (1.0 / PAIRS_PER_GROUP))
        pair = cls - grp * PAIRS_PER_GROUP
        lo = jnp.where(pair >= 3.0, 1.0, 0.0) + jnp.where(pair >= 5.0, 1.0, 0.0)
        hi = jnp.where(pair == 0.0, 1.0, jnp.where(jnp.logical_or(pair == 1.0, pair == 3.0), 2.0, 3.0))
        blk_ref[0:1, :] = (grp * EXPERTS_PER_GROUP + lo).astype(I32)
        blk_ref[1:2, :] = (grp * EXPERTS_PER_GROUP + hi).astype(I32)
        nblk = pad_end[CLASS_ROWS - 1:CLASS_ROWS, :] * (1.0 / MOE_BLOCK)
        nblk_ref[...] = nblk.astype(I32)
        row = lax.broadcasted_iota(I32, (CLASS_ROWS, LANES), 0)
        lane = lax.broadcasted_iota(I32, (CLASS_ROWS, LANES), 1)
        last_blk = jnp.where(padded > 0.0, pad_end - MOE_BLOCK, -1.0)
        per_class = jnp.sum(jnp.where(row == lane, last_blk, 0.0), axis=0, keepdims=True)
        lane1 = lane[0:1, :]
        tail_blk = nblk + (lane1 - CLASS_ROWS).astype(F32)
        tail = jnp.where(tail_blk < nb_total, tail_blk * MOE_BLOCK, -1.0)
        zrow = jnp.where(lane1 < CLASS_ROWS, per_class, jnp.where(lane1 < 2 * CLASS_ROWS, tail, -1.0))
        zrow_ref[...] = zrow.astype(I32)

    @pl.when(phase == 1)
    def _():
        excl = _cumsum_lanes(hits) - hits
        pos = run_ref[:, 0:1] + excl
        dest_ref[...] = jnp.sum(jnp.where(m0, pos, 0.0), axis=0, keepdims=True).astype(I32)
        run_ref[...] += tile_cnt


def _positions(cls, n_blocks):
    n = cls.shape[1]
    t = min(POS_TILE, n)
    nb_pad = -(-n_blocks // LANES) * LANES
    return pl.pallas_call(
        functools.partial(_positions_body, nb_total=n_blocks),
        grid=(2, n // t),
        in_specs=[pl.BlockSpec((1, t), lambda p, i: (0, i))],
        out_specs=[pl.BlockSpec((1, t), lambda p, i: (0, i * p)),
                   pl.BlockSpec((2, nb_pad), lambda p, i: (0, 0)),
                   pl.BlockSpec((1, LANES), lambda p, i: (0, 0)),
                   pl.BlockSpec((1, LANES), lambda p, i: (0, 0))],
        out_shape=[jax.ShapeDtypeStruct((1, n), I32),
                   jax.ShapeDtypeStruct((2, nb_pad), I32),
                   jax.ShapeDtypeStruct((1, LANES), I32),
                   jax.ShapeDtypeStruct((1, LANES), I32)],
        scratch_shapes=[pltpu.VMEM((CLASS_ROWS, LANES), F32), pltpu.VMEM((CLASS_ROWS, LANES), F32)],
        compiler_params=_cparams("arbitrary", "arbitrary"),
        name="moe_positions",
    )(cls)


def _row_copy(src_ref, src_row, dst_ref, dst_row, sem):
    return pltpu.make_async_copy(src_ref.at[pl.ds(src_row, 1), :], dst_ref.at[pl.ds(dst_row, 1), :], sem)


def _dispatch_body(dest_ref, zrow_ref, h_ref, xs_ref, zero_ref, sem, zsem):
    t = dest_ref.shape[1]

    @pl.when(pl.program_id(0) == 0)
    def _():
        zero_ref[...] = jnp.zeros_like(zero_ref)

        def zero_copy(k):
            start = pl.multiple_of(zrow_ref[0, k], MOE_BLOCK)
            return pltpu.make_async_copy(zero_ref, xs_ref.at[pl.ds(start, MOE_BLOCK), :], zsem)

        for k in range(2 * CLASS_ROWS):
            @pl.when(zrow_ref[0, k] >= 0)
            def _():
                zero_copy(k).start()

        for k in range(2 * CLASS_ROWS):
            @pl.when(zrow_ref[0, k] >= 0)
            def _():
                zero_copy(k).wait()

    def issue(jo, carry):
        for k in range(DMA_UNROLL):
            j = jo * DMA_UNROLL + k
            _row_copy(h_ref, j, xs_ref, dest_ref[0, j], sem).start()
        return carry

    lax.fori_loop(0, t // DMA_UNROLL, issue, 0)
    pltpu.make_async_copy(h_ref, xs_ref.at[pl.ds(0, t), :], sem).wait()


def _dispatch(dest, zrow, h2, n_rows):
    n, d = h2.shape
    t = min(DMA_TILE, n)
    return pl.pallas_call(
        _dispatch_body,
        grid=(n // t,),
        in_specs=[pl.BlockSpec((1, t), lambda i: (0, i), memory_space=pltpu.SMEM),
                  pl.BlockSpec((1, LANES), lambda i: (0, 0), memory_space=pltpu.SMEM),
                  pl.BlockSpec((t, d), lambda i: (i, 0))],
        out_specs=pl.BlockSpec(memory_space=pl.ANY),
        out_shape=jax.ShapeDtypeStruct((n_rows, d), F32),
        scratch_shapes=[pltpu.VMEM((MOE_BLOCK, d), F32), pltpu.SemaphoreType.DMA(()), pltpu.SemaphoreType.DMA(())],
        compiler_params=_cparams("arbitrary"),
        name="moe_dispatch",
    )(dest, zrow, h2)


def _experts_body(blk_lo_ref, blk_hi_ref, nblk_ref, xs_ref, wg_lo_ref, wu_lo_ref, wd_lo_ref,
                  wg_hi_ref, wu_hi_ref, wd_hi_ref, y_ref):
    used = pl.program_id(0) < nblk_ref[0]
    d = y_ref.shape[1]

    @pl.when(used)
    def _():
        x = xs_ref[:, :d].astype(BF16)
        wts = xs_ref[:, d:]

        def ffn(wg_ref, wu_ref, wd_ref):
            hid = _silu(jnp.dot(x, wg_ref[0], preferred_element_type=F32)) * jnp.dot(x, wu_ref[0], preferred_element_type=F32)
            return jnp.dot(hid.astype(BF16), wd_ref[0], preferred_element_type=F32)

        y_ref[...] = ffn(wg_lo_ref, wu_lo_ref, wd_lo_ref) * wts[:, 0:1] + ffn(wg_hi_ref, wu_hi_ref, wd_hi_ref) * wts[:, 1:2]

    @pl.when(jnp.logical_not(used))
    def _():
        y_ref[...] = jnp.zeros_like(y_ref)


def _experts(blk, nblk, xs, wg, wu, wd):
    n_rows = xs.shape[0]
    d = wg.shape[1]
    nb = n_rows // MOE_BLOCK

    def live(b, nbl):
        return jnp.minimum(b, nbl[0] - 1)

    rows = lambda b, lo, hi, nbl: (live(b, nbl), 0)
    w_lo = lambda b, lo, hi, nbl: (lo[live(b, nbl)], 0, 0)
    w_hi = lambda b, lo, hi, nbl: (hi[live(b, nbl)], 0, 0)
    return pl.pallas_call(
        _experts_body,
        grid_spec=pltpu.PrefetchScalarGridSpec(
            num_scalar_prefetch=3,
            grid=(nb,),
            in_specs=[pl.BlockSpec((MOE_BLOCK, xs.shape[1]), rows),
                      pl.BlockSpec((1, d, D_EXPERT), w_lo),
                      pl.BlockSpec((1, d, D_EXPERT), w_lo),
                      pl.BlockSpec((1, D_EXPERT, d), w_lo),
                      pl.BlockSpec((1, d, D_EXPERT), w_hi),
                      pl.BlockSpec((1, d, D_EXPERT), w_hi),
                      pl.BlockSpec((1, D_EXPERT, d), w_hi)],
            out_specs=pl.BlockSpec((MOE_BLOCK, d), lambda b, lo, hi, nbl: (b, 0))),
        out_shape=jax.ShapeDtypeStruct((n_rows, d), F32),
        compiler_params=_cparams("arbitrary"),
        name="moe_experts",
    )(blk[0], blk[1], nblk, xs, wg, wu, wd, wg, wu, wd)


def _combine_body(dest_ref, x_ref, ga_ref, fw_ref, yb_ref, o_ref, r_ref, sem, *, final_norm):
    t = dest_ref.shape[1]

    def issue(jo, carry):
        for k in range(DMA_UNROLL):
            j = jo * DMA_UNROLL + k
            _row_copy(yb_ref, dest_ref[0, j], r_ref, j, sem).start()
        return carry

    lax.fori_loop(0, t // DMA_UNROLL, issue, 0)
    pltpu.make_async_copy(yb_ref.at[pl.ds(0, t), :], r_ref, sem).wait()
    x = x_ref[...] + _seq_rows(ga_ref, t) * r_ref[...]
    if final_norm:
        x = x * lax.rsqrt(jnp.mean(x * x, axis=-1, keepdims=True) + EPS) * fw_ref[...]
    o_ref[...] = x


def _combine(dest, x2, ga2, fw, yb, seq_len, final_norm):
    n, d = x2.shape
    t = min(DMA_TILE, n)
    row = lambda i: (i, 0)
    return pl.pallas_call(
        functools.partial(_combine_body, final_norm=final_norm),
        grid=(n // t,),
        in_specs=[pl.BlockSpec((1, t), lambda i: (0, i), memory_space=pltpu.SMEM),
                  pl.BlockSpec((t, d), row),
                  _seq_block(t, seq_len, d),
                  pl.BlockSpec((1, d), lambda i: (0, 0)),
                  pl.BlockSpec(memory_space=pl.ANY)],
        out_specs=pl.BlockSpec((t, d), row),
        out_shape=jax.ShapeDtypeStruct((n, d), F32),
        scratch_shapes=[pltpu.VMEM((t, d), F32), pltpu.SemaphoreType.DMA(())],
        compiler_params=_cparams("arbitrary"),
        name="moe_combine",
    )(dest, x2, ga2, fw, yb)


def _lane_vec(vals, lane):
    return jnp.zeros((1, LANES), F32).at[0, lane:lane + vals.shape[0]].set(vals.astype(F32))


def _w_in_layout_body(w_ref, small_ref, o_ref):
    for name, i in WIDE_SEGMENTS:
        dst, width = COLS[name]
        o_ref[0, :, dst:dst + width] = w_ref[0, :, IN_OFFSETS[i]:IN_OFFSETS[i] + width].astype(BF16)
    dst, width = COLS["small"]
    o_ref[0, :, dst:dst + width] = small_ref[0]


def _w_in_layout(w_in):
    depth, d, in_d = w_in.shape
    small = jnp.zeros((depth, d, LANES), BF16)
    for i, lane in GATE_COLUMNS:
        small = small.at[:, :, lane:lane + IN_SIZES[i]].set(w_in[:, :, IN_OFFSETS[i]:IN_OFFSETS[i] + IN_SIZES[i]].astype(BF16))
    tr = 256
    return pl.pallas_call(
        _w_in_layout_body,
        grid=(depth, d // tr),
        in_specs=[pl.BlockSpec((1, tr, in_d), lambda l, i: (l, i, 0)),
                  pl.BlockSpec((1, tr, LANES), lambda l, i: (l, i, 0))],
        out_specs=pl.BlockSpec((1, tr, IN_R), lambda l, i: (l, i, 0)),
        out_shape=jax.ShapeDtypeStruct((depth, d, IN_R), BF16),
        compiler_params=_cparams("parallel", "parallel"),
        name="w_in_layout",
    )(w_in, small)


def _layer_params(p, l, w_r):
    w_out = p["w_out"][l].astype(BF16)
    return dict(
        w_r=w_r[l],
        wa=w_out[:SSD_D], wb=w_out[SSD_D:SSD_D + GMLP_D], wc=w_out[SSD_D + GMLP_D:],
        norm_mix=p["norm_mix"][l][None, :], norm_ffn=p["norm_ffn"][l][None, :],
        ssd_cw=p["ssd_conv_w"][l], ssd_cb=p["ssd_conv_b"][l][None, :],
        ssd_dtb=_lane_vec(p["ssd_dt_bias"][l], SSD_DT_LANE), ssd_alog=_lane_vec(p["ssd_a_log"][l], SSD_DT_LANE),
        ssd_dvec=jnp.repeat(p["ssd_d"][l].astype(F32), SSD_HEAD_DIM)[None, :], ssd_nw=p["ssd_norm_w"][l][None, :],
        lng=p["gmlp_ln_g"][l][None, :], lnb=p["gmlp_ln_b"][l][None, :], ws=p["gmlp_ws"][l], bs=p["gmlp_bs"][l],
        gdn_cw=p["gdn_conv_w"][l],
        gdn_dtb=_lane_vec(p["gdn_dt_bias"][l], GDN_A_LANE), gdn_alog=_lane_vec(p["gdn_a_log"][l], GDN_A_LANE),
        gdn_nw=p["gdn_norm_w"][l][None, :],
        wg=p["w_gate"][l].astype(BF16), wu=p["w_up"][l].astype(BF16), wd=p["w_down"][l].astype(BF16),
    )


def _trunk(x, mod, ssd_h0, ssd_buf0, gdn_s0, gdn_buf0, p, layers):
    bsz, seq_len, d = x.shape
    n = bsz * seq_len
    x2 = x.reshape(n, d)
    n_rows = -(-n // MOE_BLOCK) * MOE_BLOCK + N_CLASSES * MOE_BLOCK
    qg = min(GMLP_CHUNK, seq_len)
    wrt = p["w_router"].T.astype(BF16)
    br = p["b_router"].astype(F32)[:, None]
    fw = p["norm_final"][None, :]
    outs = ([], [], [], [], [])
    depth = len(layers)
    for l, lp in enumerate(layers):
        sh1, sc1, ga1, sh2, sc2, ga2 = (m[:, None, :] for m in jnp.split(mod[l], 6, axis=-1))
        proj = _inproj(x2, sh1, sc1, lp["norm_mix"], lp["w_r"], seq_len)
        ya, ssd_ht, ssd_buf = _ssd(proj, jnp.swapaxes(ssd_h0[l], -1, -2), ssd_buf0[l], lp["ssd_cw"], lp["ssd_cb"],
                                   lp["ssd_dtb"], lp["ssd_alog"], lp["ssd_dvec"], lp["ssd_nw"], bsz, seq_len)
        yb, vn = _gmlp(proj, lp["lng"], lp["lnb"], lp["ws"][:, :qg, :qg], lp["bs"][:, :qg].T, bsz, seq_len)
        yc, gdn_s, gdn_buf = _gdn(proj, gdn_s0[l], gdn_buf0[l], lp["gdn_cw"], lp["gdn_dtb"], lp["gdn_alog"],
                                  lp["gdn_nw"], bsz, seq_len)
        x2, h2w, cls = _outproj_route(x2, ya, yb, yc, lp["wa"], lp["wb"], lp["wc"], ga1, lp["norm_ffn"],
                                      sc2, sh2, wrt, br, seq_len)
        dest, blk, nblk, zrow = _positions(cls, n_rows // MOE_BLOCK)
        xs = _dispatch(dest, zrow, h2w, n_rows)
        ye = _experts(blk, nblk[0, :1], xs, lp["wg"], lp["wu"], lp["wd"])
        x2 = _combine(dest, x2, ga2, fw, ye, seq_len, final_norm=(l == depth - 1))
        for lst, s in zip(outs, (jnp.swapaxes(ssd_ht, -1, -2), ssd_buf, gdn_s, gdn_buf, vn.reshape(bsz, seq_len, GMLP_D))):
            lst.append(s)
    return (x2.reshape(bsz, seq_len, d),) + tuple(jnp.stack(o) for o in outs)


def kernel(x_prompt, x_sample, state_ssd, state_ssd_conv, state_gdn, state_gdn_conv, c_prompt, c_sample,
           w_ada, b_ada, norm_mix, norm_ffn, norm_final, w_in, w_out,
           ssd_conv_w, ssd_conv_b, ssd_dt_bias, ssd_a_log, ssd_d, ssd_norm_w,
           gmlp_ln_g, gmlp_ln_b, gmlp_ws, gmlp_bs,
           gdn_conv_w, gdn_a_log, gdn_dt_bias, gdn_norm_w,
           w_router, b_router, w_gate, w_up, w_down):
    p = dict(norm_mix=norm_mix, norm_ffn=norm_ffn, norm_final=norm_final, w_in=w_in, w_out=w_out,
             ssd_conv_w=ssd_conv_w, ssd_conv_b=ssd_conv_b, ssd_dt_bias=ssd_dt_bias, ssd_a_log=ssd_a_log,
             ssd_d=ssd_d, ssd_norm_w=ssd_norm_w, gmlp_ln_g=gmlp_ln_g, gmlp_ln_b=gmlp_ln_b, gmlp_ws=gmlp_ws,
             gmlp_bs=gmlp_bs, gdn_conv_w=gdn_conv_w, gdn_a_log=gdn_a_log, gdn_dt_bias=gdn_dt_bias,
             gdn_norm_w=gdn_norm_w, w_router=w_router, b_router=b_router, w_gate=w_gate, w_up=w_up, w_down=w_down)
    depth = w_in.shape[0]
    w_r = _w_in_layout(w_in)
    layers = [_layer_params(p, l, w_r) for l in range(depth)]
    bp = x_prompt.shape[0]
    mod = _ada_mod(jnp.concatenate([c_prompt, c_sample], axis=0), w_ada.astype(BF16), b_ada)
    z_ssd = jnp.zeros((depth, bp) + state_ssd.shape[2:], state_ssd.dtype)
    z_ssd_conv = jnp.zeros((depth, bp) + state_ssd_conv.shape[2:], x_prompt.dtype)
    z_gdn = jnp.zeros((depth, bp) + state_gdn.shape[2:], state_gdn.dtype)
    z_gdn_conv = jnp.zeros((depth, bp) + state_gdn_conv.shape[2:], x_prompt.dtype)
    y_p, p_ssd, p_ssd_conv, p_gdn, p_gdn_conv, _ = _trunk(
        x_prompt, mod[:, :bp], z_ssd, z_ssd_conv, z_gdn, z_gdn_conv, p, layers)
    y_s, s_ssd, s_ssd_conv, s_gdn, s_gdn_conv, s_gmlp_v = _trunk(
        x_sample, mod[:, bp:], state_ssd, state_ssd_conv, state_gdn, state_gdn_conv, p, layers)
    return (y_p, y_s, p_ssd, p_ssd_conv, p_gdn, p_gdn_conv, s_ssd, s_ssd_conv, s_gdn, s_gdn_conv, s_gmlp_v)
```

```python
import functools

import jax
import jax.numpy as jnp
from jax import lax
from jax.experimental import pallas as pl
from jax.experimental.pallas import tpu as pltpu

F32 = jnp.float32
BF16 = jnp.bfloat16
I32 = jnp.int32

LANES = 128
SUBLANES = 8
VMEM_LIMIT_BYTES = 56 * 1024 * 1024

D_MODEL = 1024
EPS = 1e-6
CHUNK = 64
CONV_W = 4
SSD_HEADS = 6
SSD_HEAD_DIM = 64
SSD_STATE = 64
SSD_GROUPS = 2
SSD_D = SSD_HEADS * SSD_HEAD_DIM
SSD_CONV_DIM = SSD_D + 2 * SSD_GROUPS * SSD_STATE
GMLP_GROUPS = 4
GMLP_GROUP_DIM = 64
GMLP_D = GMLP_GROUPS * GMLP_GROUP_DIM
GMLP_CHUNK = 128
GDN_HEADS = 6
GDN_KEY_DIM = 64
GDN_VAL_DIM = 64
GDN_QK_D = GDN_HEADS * GDN_KEY_DIM
GDN_V_D = GDN_HEADS * GDN_VAL_DIM
GDN_CONV_DIM = 2 * GDN_QK_D + GDN_V_D
GDN_PAIRS = GDN_HEADS // 2
SSD_PAIRS = SSD_HEADS // 2
assert 2 * SSD_HEAD_DIM == 2 * SSD_STATE == 2 * GDN_KEY_DIM == 2 * GDN_VAL_DIM == 2 * GMLP_GROUP_DIM == 128
assert SSD_PAIRS == GDN_PAIRS and GMLP_GROUPS % 2 == 0
N_EXPERTS = 16
N_EXPERT_GROUPS = 4
EXPERTS_PER_GROUP = N_EXPERTS // N_EXPERT_GROUPS
D_EXPERT = 512
PAIRS_PER_GROUP = EXPERTS_PER_GROUP * (EXPERTS_PER_GROUP - 1) // 2
N_CLASSES = N_EXPERT_GROUPS * PAIRS_PER_GROUP
CLASS_ROWS = -(-N_CLASSES // SUBLANES) * SUBLANES
assert EXPERTS_PER_GROUP == 4 and 2 * CLASS_ROWS <= LANES

IN_SIZES = (SSD_CONV_DIM, SSD_D, SSD_HEADS, GMLP_D, GMLP_D, GDN_CONV_DIM, GDN_HEADS, GDN_HEADS, GDN_V_D)
IN_OFFSETS = tuple(sum(IN_SIZES[:i]) for i in range(len(IN_SIZES)))
WIDE_SEGMENTS = (("qkv", 5), ("z", 1), ("gate", 8), ("xbc", 0), ("u", 3), ("v", 4))
SSD_DT_LANE = 0
GDN_A_LANE = 8
GDN_B_LANE = 16
GATE_COLUMNS = ((2, SSD_DT_LANE), (6, GDN_A_LANE), (7, GDN_B_LANE))
COLS = {}
for _name, _i in WIDE_SEGMENTS:
    COLS[_name] = (sum(w for _, w in COLS.values()), IN_SIZES[_i])
COLS["small"] = (sum(w for _, w in COLS.values()), LANES)
IN_R = sum(w for _, w in COLS.values())
assert all(off % w == 0 for off, w in COLS.values())
TAIL_ROW = SUBLANES - (CONV_W - 1)

MOE_BLOCK = 256
ROW_TILE = 512
POS_TILE = 2048
GDN_STEP_CHUNKS = 8
SSD_STEP_CHUNKS = 4
GMLP_STEP_CHUNKS = 4
DMA_TILE = 512
DMA_UNROLL = 8


def _cparams(*sem):
    return pltpu.CompilerParams(dimension_semantics=sem, vmem_limit_bytes=VMEM_LIMIT_BYTES)


def _bdot(a, b):
    return jnp.dot(a.astype(BF16), b.astype(BF16), preferred_element_type=F32)


def _bdot_nt(a, b):
    return lax.dot_general(a.astype(BF16), b.astype(BF16), (((1,), (1,)), ((), ())), preferred_element_type=F32)


def _bdot_tn(a, b):
    return lax.dot_general(a.astype(BF16), b.astype(BF16), (((0,), (0,)), ((), ())), preferred_element_type=F32)


def _tn_split(a, b):
    return _split_dot(a, b, (((0,), (0,)), ((), ())))


def _split_dot(a, b, dims):
    a_hi, b_hi = a.astype(BF16), b.astype(BF16)
    a_lo = (a - a_hi.astype(F32)).astype(BF16)
    b_lo = (b - b_hi.astype(F32)).astype(BF16)
    dot = lambda x, y: lax.dot_general(x, y, dims, preferred_element_type=F32)
    return dot(a_hi, b_hi) + (dot(a_lo, b_hi) + dot(a_hi, b_lo))


def _sigmoid(x):
    return 1.0 / (1.0 + jnp.exp(-x))


def _silu(x):
    return x * _sigmoid(x)


def _softplus(x):
    return jnp.maximum(x, 0.0) + jnp.log1p(jnp.exp(-jnp.abs(x)))


def _gelu_tanh(x):
    return 0.5 * x * (1.0 + jnp.tanh(0.7978845608028654 * (x + 0.044715 * (x * x * x))))


def _cumsum_rows(a):
    n = a.shape[0]
    row = lax.broadcasted_iota(I32, a.shape, 0)
    s = 1
    while s < n:
        a = a + jnp.where(row >= s, pltpu.roll(a, s, axis=0), jnp.zeros_like(a))
        s *= 2
    return a


def _cumsum_lanes(a):
    n = a.shape[1]
    col = lax.broadcasted_iota(I32, a.shape, 1)
    s = 1
    while s < n:
        a = a + jnp.where(col >= s, pltpu.roll(a, s, axis=1), jnp.zeros_like(a))
        s *= 2
    return a


def _rows_to_lanes(a):
    q = a.shape[0]
    if q < LANES:
        a = jnp.concatenate([a, jnp.zeros((LANES - q, LANES), a.dtype)], axis=0)
    return a.T[:, :q]


def _ada_body(c_ref, w_ref, b_ref, o_ref):
    o_ref[0] = _bdot(_silu(c_ref[...]), w_ref[0]) + b_ref[0]


def _ada_mod(c_all, w_ada, b_ada):
    depth, d, n6 = w_ada.shape
    bc = c_all.shape[0]
    tn = 1536
    return pl.pallas_call(
        _ada_body,
        grid=(depth, n6 // tn),
        in_specs=[pl.BlockSpec((bc, d), lambda l, j: (0, 0)),
                  pl.BlockSpec((1, d, tn), lambda l, j: (l, 0, j)),
                  pl.BlockSpec((1, 1, tn), lambda l, j: (l, 0, j))],
        out_specs=pl.BlockSpec((1, bc, tn), lambda l, j: (l, 0, j)),
        out_shape=jax.ShapeDtypeStruct((depth, bc, n6), F32),
        compiler_params=_cparams("parallel", "parallel"),
        name="ada_mod",
    )(c_all, w_ada, b_ada.reshape(depth, 1, n6))


def _seq_block(tm, seq_len, d):
    if tm <= seq_len:
        per_seq = seq_len // tm
        return pl.BlockSpec((1, 1, d), lambda i: (i // per_seq, 0, 0))
    return pl.BlockSpec((tm // seq_len, 1, d), lambda i: (i, 0, 0))


def _seq_rows(ref, tm):
    s = ref.shape[0]
    if s == 1:
        return ref[0]
    return jnp.concatenate([jnp.broadcast_to(ref[k], (tm // s, ref.shape[2])) for k in range(s)], axis=0)


def _mod_rms(x, nw, sc, sh):
    ms = jnp.mean(x * x, axis=-1, keepdims=True)
    return (x * lax.rsqrt(ms + EPS) * nw) * (1.0 + sc) + sh


def _inproj_body(x_ref, sh_ref, sc_ref, nw_ref, w_ref, o_ref):
    tm = x_ref.shape[0]
    hb = _mod_rms(x_ref[...], nw_ref[...], _seq_rows(sc_ref, tm), _seq_rows(sh_ref, tm)).astype(BF16)
    o_ref[...] = jnp.dot(hb, w_ref[...], preferred_element_type=F32)


def _inproj(x2, sh, sc, nw, w_r, seq_len):
    n, d = x2.shape
    tm = min(ROW_TILE, n)
    row = lambda i: (i, 0)
    return pl.pallas_call(
        _inproj_body,
        grid=(n // tm,),
        in_specs=[pl.BlockSpec((tm, d), row),
                  _seq_block(tm, seq_len, d),
                  _seq_block(tm, seq_len, d),
                  pl.BlockSpec((1, d), lambda i: (0, 0)),
                  pl.BlockSpec((d, IN_R), lambda i: (0, 0))],
        out_specs=pl.BlockSpec((tm, IN_R), row),
        out_shape=jax.ShapeDtypeStruct((n, IN_R), F32),
        compiler_params=_cparams("parallel"),
        name="inproj",
    )(x2, sh, sc, nw, w_r)


def _proj_cols(name, rows, row_index):
    off, width = COLS[name]
    return pl.BlockSpec((rows, width), lambda *g: (row_index(*g), off // width))


def _causal_conv_step(x_ref, buf0_ref, bufout_ref, xp_ref, cw_ref, first):
    q = x_ref.shape[0]

    @pl.when(first)
    def _():
        xp_ref[TAIL_ROW:SUBLANES, :] = buf0_ref[0]

    xp_ref[SUBLANES:SUBLANES + q, :] = x_ref[...]
    acc = cw_ref[0:1, :] * xp_ref[TAIL_ROW:TAIL_ROW + q, :]
    for k in range(1, CONV_W):
        acc = acc + cw_ref[k:k + 1, :] * xp_ref[TAIL_ROW + k:TAIL_ROW + k + q, :]
    tail = xp_ref[TAIL_ROW + q:SUBLANES + q, :]
    bufout_ref[0] = tail
    xp_ref[TAIL_ROW:SUBLANES, :] = tail
    return acc


def _decay_terms(g):
    q = g.shape[0]
    cum = _cumsum_rows(g)
    cum_t = _rows_to_lanes(cum)
    last = cum[q - 1:q, :]
    return cum, cum_t, jnp.exp(cum), jnp.exp(last - cum), jnp.exp(last)


def _ssd_body(xbc_ref, z_ref, sm_ref, h0_ref, buf0_ref, cw_ref, cb_ref, dtb_ref, alog_ref, dvec_ref, nw_ref,
              y_ref, hout_ref, bufout_ref, xp_ref, ht_ref):
    t = xbc_ref.shape[0]
    q = min(CHUNK, t)
    first = pl.program_id(1) == 0

    @pl.when(first)
    def _():
        ht_ref[...] = h0_ref[0]

    xa = _silu(_causal_conv_step(xbc_ref, buf0_ref, bufout_ref, xp_ref, cw_ref, first) + cb_ref[...])
    dt = _softplus(sm_ref[...] + dtb_ref[...])
    a_all = dt * (-jnp.exp(alog_ref[...]))
    causal = lax.broadcasted_iota(I32, (q, LANES), 0) >= (lax.broadcasted_iota(I32, (q, LANES), 1) & (q - 1))
    lo = _low_half((q, LANES))
    b_off, c_off = SSD_D, SSD_D + SSD_GROUPS * SSD_STATE
    y_loc, st, ce, ed = {}, {}, {}, {}
    for c in range(t // q):
        rows = slice(c * q, (c + 1) * q)
        cum = _cumsum_rows(a_all[rows])
        cum_rows = jnp.concatenate([cum, pltpu.roll(cum, LANES - 1, axis=1)], axis=0).T
        last = cum[q - 1:q, :]
        ecum, wend, elast = jnp.exp(cum), jnp.exp(last - cum), jnp.exp(last)
        bm = xa[rows, b_off:b_off + LANES]
        cm = xa[rows, c_off:c_off + LANES]
        bm_sw = pltpu.roll(bm, SSD_STATE, axis=1)
        cm_sw = pltpu.roll(cm, SSD_STATE, axis=1)
        b_pairs = [jnp.where(lo, bm, bm_sw), bm, jnp.where(lo, bm_sw, bm)]
        c_pairs = [jnp.where(lo, cm, cm_sw), cm, jnp.where(lo, cm_sw, cm)]
        for p in range(SSD_PAIRS):
            lane = SSD_DT_LANE + 2 * p
            ls = slice(p * LANES, (p + 1) * LANES)
            xh = xa[rows, ls]
            xdt = xh * _pair_cols(dt[rows], lane)
            e = jnp.exp(_pair_cols(cum, lane) - cum_rows[lane:lane + 1, :])
            cb = lax.dot_general(c_pairs[p].astype(BF16), _block_diag(b_pairs[p].astype(BF16)),
                                 (((1,), (1,)), ((), ())), preferred_element_type=F32)
            y_loc[c, p] = _pdot(cb * jnp.where(causal, e, 0.0), xdt) + dvec_ref[:, ls] * xh
            st[c, p] = _tn_split(b_pairs[p] * _pair_cols(wend, lane), xdt)
            ce[c, p] = c_pairs[p] * _pair_cols(ecum, lane)
            ed[c, p] = _pair_cols(elast, lane)
    h_cur = [ht_ref[p] for p in range(SSD_PAIRS)]
    half = SSD_D // SSD_GROUPS
    for c in range(t // q):
        rows = slice(c * q, (c + 1) * q)
        yz = []
        for p in range(SSD_PAIRS):
            ls = slice(p * LANES, (p + 1) * LANES)
            y = y_loc[c, p] + _pdot(ce[c, p], h_cur[p])
            h_cur[p] = h_cur[p] * ed[c, p] + jnp.where(lo, st[c, p][:q], st[c, p][q:])
            yz.append(y * _silu(z_ref[rows, ls]))
        sq = [v * v for v in yz]
        row_sum = lambda v: jnp.sum(v, axis=-1, keepdims=True)
        ss0 = row_sum(sq[0]) + row_sum(jnp.where(lo, sq[1], 0.0))
        ss1 = row_sum(jnp.where(lo, 0.0, sq[1])) + row_sum(sq[2])
        r0 = lax.rsqrt(ss0 * (1.0 / half) + EPS)
        r1 = lax.rsqrt(ss1 * (1.0 / half) + EPS)
        for p, scale in enumerate((r0, jnp.where(lo, r0, r1), r1)):
            ls = slice(p * LANES, (p + 1) * LANES)
            y_ref[rows, ls] = yz[p] * scale * nw_ref[:, ls]
    for p in range(SSD_PAIRS):
        ht_ref[p] = h_cur[p]
        hout_ref[0, p] = h_cur[p]


def _ssd(proj, h0t, buf0, cw, cb, dtb, alog, dvec, nw, bsz, seq_len):
    t = min(CHUNK * SSD_STEP_CHUNKS, seq_len)
    nc = seq_len // t
    n = bsz * seq_len
    tok = lambda b, c: (b * nc + c, 0)
    tok_row = lambda b, c: b * nc + c
    per_b4 = lambda b, c: (b, 0, 0, 0)
    per_b3 = lambda b, c: (b, 0, 0)
    const = lambda b, c: (0, 0)
    state = (SSD_PAIRS, SSD_STATE, 2 * SSD_HEAD_DIM)
    y, h_pairs, buf = pl.pallas_call(
        _ssd_body,
        grid=(bsz, nc),
        in_specs=[_proj_cols("xbc", t, tok_row),
                  _proj_cols("z", t, tok_row),
                  _proj_cols("small", t, tok_row),
                  pl.BlockSpec((1,) + state, per_b4),
                  pl.BlockSpec((1, CONV_W - 1, SSD_CONV_DIM), per_b3),
                  pl.BlockSpec((CONV_W, SSD_CONV_DIM), const),
                  pl.BlockSpec((1, SSD_CONV_DIM), const),
                  pl.BlockSpec((1, LANES), const),
                  pl.BlockSpec((1, LANES), const),
                  pl.BlockSpec((1, SSD_D), const),
                  pl.BlockSpec((1, SSD_D), const)],
        out_specs=[pl.BlockSpec((t, SSD_D), tok),
                   pl.BlockSpec((1,) + state, per_b4),
                   pl.BlockSpec((1, CONV_W - 1, SSD_CONV_DIM), per_b3)],
        out_shape=[jax.ShapeDtypeStruct((n, SSD_D), F32),
                   jax.ShapeDtypeStruct((bsz,) + state, F32),
                   jax.ShapeDtypeStruct((bsz, CONV_W - 1, SSD_CONV_DIM), F32)],
        scratch_shapes=[pltpu.VMEM((SUBLANES + t, SSD_CONV_DIM), F32),
                        pltpu.VMEM(state, F32)],
        compiler_params=_cparams("parallel", "arbitrary"),
        name="ssd",
    )(proj, proj, proj, _heads_to_pairs(h0t), buf0, cw, cb, dtb, alog, dvec, nw)
    return y, _pairs_to_heads(h_pairs), buf


def _gmlp_body(u_ref, v_ref, lng_ref, lnb_ref, ws_ref, bst_ref, y_ref, vn_ref):
    t = u_ref.shape[0]
    q = ws_ref.shape[1]
    lower = lax.broadcasted_iota(I32, (q, q), 0) >= lax.broadcasted_iota(I32, (q, q), 1)
    w_low = [jnp.where(lower, ws_ref[g], 0.0).astype(BF16) for g in range(GMLP_GROUPS)]
    lo = _low_half((q, LANES))
    inv_dim = 1.0 / GMLP_GROUP_DIM
    work = [(c, p) for c in range(t // q) for p in range(GMLP_GROUPS // 2)]
    vnb = {}
    for c, p in work:
        rows, ls = slice(c * q, (c + 1) * q), slice(p * LANES, (p + 1) * LANES)
        gv = _gelu_tanh(v_ref[rows, ls])
        dv = gv - _half_sums(gv) * inv_dim
        vn = dv * lax.rsqrt(_half_sums(dv * dv) * inv_dim + EPS) * lng_ref[:, ls] + lnb_ref[:, ls]
        vn_ref[rows, ls] = vn
        vnb[c, p] = vn.astype(BF16)
    zero = jnp.zeros((q, LANES), BF16)
    mixed = {(c, p): jnp.dot(w_low[2 * p], jnp.where(lo, vnb[c, p], zero), preferred_element_type=F32)
             + jnp.dot(w_low[2 * p + 1], jnp.where(lo, zero, vnb[c, p]), preferred_element_type=F32) for c, p in work}
    for c, p in work:
        rows, ls = slice(c * q, (c + 1) * q), slice(p * LANES, (p + 1) * LANES)
        y_ref[rows, ls] = _gelu_tanh(u_ref[rows, ls]) * (mixed[c, p] + _pair_cols(bst_ref[...], 2 * p))


def _gmlp(proj, lng, lnb, ws, bst, bsz, seq_len):
    q = ws.shape[1]
    t = min(q * GMLP_STEP_CHUNKS, seq_len)
    n = bsz * seq_len
    tok = lambda i: (i, 0)
    return pl.pallas_call(
        _gmlp_body,
        grid=(n // t,),
        in_specs=[_proj_cols("u", t, lambda i: i),
                  _proj_cols("v", t, lambda i: i),
                  pl.BlockSpec((1, GMLP_D), lambda i: (0, 0)),
                  pl.BlockSpec((1, GMLP_D), lambda i: (0, 0)),
                  pl.BlockSpec((GMLP_GROUPS, q, q), lambda i: (0, 0, 0)),
                  pl.BlockSpec((q, GMLP_GROUPS), lambda i: (0, 0))],
        out_specs=[pl.BlockSpec((t, GMLP_D), tok), pl.BlockSpec((t, GMLP_D), tok)],
        out_shape=[jax.ShapeDtypeStruct((n, GMLP_D), F32), jax.ShapeDtypeStruct((n, GMLP_D), F32)],
        compiler_params=_cparams("parallel"),
        name="gmlp",
    )(proj, proj, lng, lnb, ws, bst)


def _low_half(shape):
    return lax.broadcasted_iota(I32, shape, 1) < GDN_KEY_DIM


def _pair_cols(m, lane):
    shape = (m.shape[0], LANES)
    return jnp.where(_low_half(shape), m[:, lane:lane + 1], m[:, lane + 1:lane + 2])


def _half_sums(s):
    lo = _low_half(s.shape)
    s_lo = jnp.sum(jnp.where(lo, s, 0.0), axis=-1, keepdims=True)
    s_hi = jnp.sum(jnp.where(lo, 0.0, s), axis=-1, keepdims=True)
    return jnp.where(lo, s_lo, s_hi)


def _l2_halves(x):
    return x * lax.rsqrt(_half_sums(x * x) + EPS)


def _block_diag(y):
    q = y.shape[0]
    yy = jnp.concatenate([y, y], axis=0)
    same = (lax.broadcasted_iota(I32, yy.shape, 0) < q) == _low_half(yy.shape)
    return jnp.where(same, yy, jnp.zeros_like(yy))


def _pdot(x, y):
    return jnp.dot(x.astype(BF16), _block_diag(y.astype(BF16)), preferred_element_type=F32)


def _unit_lower_inverse_minus_identity(mats):
    q = mats[0].shape[0]
    ii = lax.broadcasted_iota(I32, (q, LANES), 0)
    jj = lax.broadcasted_iota(I32, (q, LANES), 1) & (q - 1)
    base = (ii >> 1) == (jj >> 1)
    rs = [-jnp.where(base, a, 0.0) for a in mats]
    level = 1
    while (1 << level) < q:
        in_pair = (ii >> (level + 1)) == (jj >> (level + 1))
        off_diag = (ii >> level) != (jj >> level)
        mask = jnp.logical_and(in_pair, off_diag)
        ams = [jnp.where(mask, a, 0.0) for a in mats]
        ws = [am + _pdot(am, r) for am, r in zip(ams, rs)]
        rs = [r - w - _pdot(r, w) for r, w in zip(rs, ws)]
        level += 1
    return rs


def _gdn_body(qkv_ref, gg_ref, sm_ref, s0_ref, buf0_ref, cw_ref, dtb_ref, alog_ref, nw_ref,
              y_ref, sout_ref, bufout_ref, xp_ref, s_ref):
    t = qkv_ref.shape[0]
    q = min(CHUNK, t)
    first = pl.program_id(1) == 0

    @pl.when(first)
    def _():
        s_ref[...] = s0_ref[0]

    xa = _silu(_causal_conv_step(qkv_ref, buf0_ref, bufout_ref, xp_ref, cw_ref, first))
    sm = sm_ref[...]
    beta_all = _sigmoid(sm)
    g_all = -jnp.exp(alog_ref[...]) * _softplus(sm + dtb_ref[...])
    ii = lax.broadcasted_iota(I32, (q, LANES), 0)
    jj = lax.broadcasted_iota(I32, (q, LANES), 1) & (q - 1)
    lo = _low_half((q, LANES))
    chains = [(c, p) for c in range(t // q) for p in range(GDN_PAIRS)]
    qp, kp, kdec, edec, kk, a_mat, attn, vb, kbe, qe = {}, {}, {}, {}, {}, {}, {}, {}, {}, {}
    for c in range(t // q):
        rows = slice(c * q, (c + 1) * q)
        cum = _cumsum_rows(g_all[rows])
        cum_rows = jnp.concatenate([cum, pltpu.roll(cum, LANES - 1, axis=1)], axis=0).T
        last = cum[q - 1:q, :]
        ecum, kdecw, elast = jnp.exp(cum), jnp.exp(last - cum), jnp.exp(last)
        for p in range(GDN_PAIRS):
            a_lane, b_lane = GDN_A_LANE + 2 * p, GDN_B_LANE + 2 * p
            ls = slice(p * LANES, (p + 1) * LANES)
            qp[c, p] = _l2_halves(xa[rows, ls]) * (GDN_KEY_DIM ** -0.5)
            kp[c, p] = _l2_halves(xa[rows, GDN_QK_D + p * LANES:GDN_QK_D + (p + 1) * LANES])
            beta = _pair_cols(beta_all[rows], b_lane)
            kb = kp[c, p] * beta
            e = jnp.exp(_pair_cols(cum, a_lane) - cum_rows[a_lane:a_lane + 1, :])
            ecp = _pair_cols(ecum, a_lane)
            kdec[c, p] = kp[c, p] * _pair_cols(kdecw, a_lane)
            edec[c, p] = _pair_cols(elast, a_lane)
            kk[c, p] = lax.dot_general(jnp.concatenate([kb, qp[c, p]], axis=0).astype(BF16),
                                       _block_diag(kp[c, p].astype(BF16)),
                                       (((1,), (1,)), ((), ())), preferred_element_type=F32)
            a_mat[c, p] = kk[c, p][:q] * jnp.where(ii > jj, e, 0.0)
            attn[c, p] = kk[c, p][q:] * jnp.where(ii >= jj, e, 0.0)
            vb[c, p] = xa[rows, 2 * GDN_QK_D + p * LANES:2 * GDN_QK_D + (p + 1) * LANES] * beta
            kbe[c, p] = kb * ecp
            qe[c, p] = qp[c, p] * ecp
    r = dict(zip(chains, _unit_lower_inverse_minus_identity([a_mat[cp] for cp in chains])))
    u_base = {cp: vb[cp] + _pdot(r[cp], vb[cp]) for cp in chains}
    k_cd = {cp: kbe[cp] + _pdot(r[cp], kbe[cp]) for cp in chains}
    s_cur = [s_ref[p] for p in range(GDN_PAIRS)]
    for c in range(t // q):
        rows = slice(c * q, (c + 1) * q)
        ks = [jnp.dot(jnp.concatenate([k_cd[c, p], qe[c, p]], axis=0).astype(BF16),
                      _block_diag(s_cur[p].astype(BF16)), preferred_element_type=F32) for p in range(GDN_PAIRS)]
        u = [u_base[c, p] - ks[p][:q] for p in range(GDN_PAIRS)]
        au = [_pdot(attn[c, p], u[p]) for p in range(GDN_PAIRS)]
        ku = [_bdot_tn(kdec[c, p], u[p]) for p in range(GDN_PAIRS)]
        for p in range(GDN_PAIRS):
            s_cur[p] = s_cur[p] * edec[c, p] + jnp.where(lo, ku[p][:q], ku[p][q:])
            o = ks[p][q:] + au[p]
            o = o * lax.rsqrt(_half_sums(o * o) * (1.0 / GDN_VAL_DIM) + EPS) * nw_ref[...]
            ls = slice(p * LANES, (p + 1) * LANES)
            y_ref[rows, ls] = o * _silu(gg_ref[rows, ls])
    for p in range(GDN_PAIRS):
        s_ref[p] = s_cur[p]
        sout_ref[0, p] = s_cur[p]


def _heads_to_pairs(s):
    b = s.shape[0]
    s = s.reshape(b, GDN_PAIRS, 2, GDN_KEY_DIM, GDN_VAL_DIM)
    return jnp.swapaxes(s, 2, 3).reshape(b, GDN_PAIRS, GDN_KEY_DIM, 2 * GDN_VAL_DIM)


def _pairs_to_heads(s):
    b = s.shape[0]
    s = s.reshape(b, GDN_PAIRS, GDN_KEY_DIM, 2, GDN_VAL_DIM)
    return jnp.swapaxes(s, 2, 3).reshape(b, GDN_HEADS, GDN_KEY_DIM, GDN_VAL_DIM)


def _gdn(proj, s0, buf0, cw, dtb, alog, nw, bsz, seq_len):
    t = min(CHUNK * GDN_STEP_CHUNKS, seq_len)
    nc = seq_len // t
    n = bsz * seq_len
    tok = lambda b, c: (b * nc + c, 0)
    tok_row = lambda b, c: b * nc + c
    per_b4 = lambda b, c: (b, 0, 0, 0)
    per_b3 = lambda b, c: (b, 0, 0)
    const = lambda b, c: (0, 0)
    state = (GDN_PAIRS, GDN_KEY_DIM, 2 * GDN_VAL_DIM)
    y, s_pairs, buf = pl.pallas_call(
        _gdn_body,
        grid=(bsz, nc),
        in_specs=[_proj_cols("qkv", t, tok_row),
                  _proj_cols("gate", t, tok_row),
                  _proj_cols("small", t, tok_row),
                  pl.BlockSpec((1,) + state, per_b4),
                  pl.BlockSpec((1, CONV_W - 1, GDN_CONV_DIM), per_b3),
                  pl.BlockSpec((CONV_W, GDN_CONV_DIM), const),
                  pl.BlockSpec((1, LANES), const),
                  pl.BlockSpec((1, LANES), const),
                  pl.BlockSpec((1, 2 * GDN_VAL_DIM), const)],
        out_specs=[pl.BlockSpec((t, GDN_V_D), tok),
                   pl.BlockSpec((1,) + state, per_b4),
                   pl.BlockSpec((1, CONV_W - 1, GDN_CONV_DIM), per_b3)],
        out_shape=[jax.ShapeDtypeStruct((n, GDN_V_D), F32),
                   jax.ShapeDtypeStruct((bsz,) + state, F32),
                   jax.ShapeDtypeStruct((bsz, CONV_W - 1, GDN_CONV_DIM), F32)],
        scratch_shapes=[pltpu.VMEM((SUBLANES + t, GDN_CONV_DIM), F32),
                        pltpu.VMEM(state, F32)],
        compiler_params=_cparams("parallel", "arbitrary"),
        name="gdn",
    )(proj, proj, proj, _heads_to_pairs(s0), buf0, cw, dtb, alog, jnp.concatenate([nw, nw], axis=1))
    return y, _pairs_to_heads(s_pairs), buf


def _route(logits, b_col):
    aff = _sigmoid(logits)
    sel = aff + b_col
    srow = [sel[e:e + 1, :] for e in range(N_EXPERTS)]
    arow = [aff[e:e + 1, :] for e in range(N_EXPERTS)]
    gscore = []
    for g in range(N_EXPERT_GROUPS):
        a, b, c, d = srow[g * EXPERTS_PER_GROUP:(g + 1) * EXPERTS_PER_GROUP]
        hi1, lo1, hi2, lo2 = jnp.maximum(a, b), jnp.minimum(a, b), jnp.maximum(c, d), jnp.minimum(c, d)
        gscore.append(jnp.maximum(hi1, hi2) + jnp.maximum(jnp.minimum(hi1, hi2), jnp.maximum(lo1, lo2)))
    best = jnp.zeros(gscore[0].shape, I32)
    best_v = gscore[0]
    for g in range(1, N_EXPERT_GROUPS):
        upd = gscore[g] > best_v
        best = jnp.where(upd, g, best)
        best_v = jnp.where(upd, gscore[g], best_v)
    cs, ca = [], []
    for j in range(EXPERTS_PER_GROUP):
        cj, aj = srow[j], arow[j]
        for g in range(1, N_EXPERT_GROUPS):
            m = best == g
            cj = jnp.where(m, srow[g * EXPERTS_PER_GROUP + j], cj)
            aj = jnp.where(m, arow[g * EXPERTS_PER_GROUP + j], aj)
        cs.append(cj)
        ca.append(aj)
    i1 = jnp.zeros(best.shape, I32)
    v1, a1 = cs[0], ca[0]
    for j in range(1, EXPERTS_PER_GROUP):
        upd = cs[j] > v1
        i1 = jnp.where(upd, j, i1)
        v1 = jnp.where(upd, cs[j], v1)
        a1 = jnp.where(upd, ca[j], a1)
    i2 = jnp.zeros(best.shape, I32)
    v2 = jnp.full(v1.shape, -jnp.inf, F32)
    a2 = jnp.zeros(v1.shape, F32)
    for j in range(EXPERTS_PER_GROUP):
        upd = jnp.logical_and(i1 != j, cs[j] > v2)
        i2 = jnp.where(upd, j, i2)
        v2 = jnp.where(upd, cs[j], v2)
        a2 = jnp.where(upd, ca[j], a2)
    tot = a1 + a2
    swap = i2 < i1
    lo, hi = jnp.where(swap, i2, i1), jnp.where(swap, i1, i2)
    pair = jnp.where(lo == 0, hi - 1, jnp.where(lo == 1, hi + 1, PAIRS_PER_GROUP - 1))
    w1, w2 = a1 / tot, a2 / tot
    return best * PAIRS_PER_GROUP + pair, jnp.where(swap, w2, w1), jnp.where(swap, w1, w2)


def _outproj_body(x_ref, ya_ref, yb_ref, yc_ref, wa_ref, wb_ref, wc_ref, ga_ref, nw_ref, sc_ref, sh_ref,
                  wrt_ref, br_ref, xo_ref, h2_ref, cls_ref):
    m = _bdot(ya_ref[...], wa_ref[...]) + _bdot(yb_ref[...], wb_ref[...]) + _bdot(yc_ref[...], wc_ref[...])
    tm, d = x_ref.shape
    x = x_ref[...] + _seq_rows(ga_ref, tm) * m
    xo_ref[...] = x
    h2 = _mod_rms(x, nw_ref[...], _seq_rows(sc_ref, tm), _seq_rows(sh_ref, tm))
    h2_ref[:, :d] = h2
    wr = wrt_ref[...]
    wr_hi, h_hi = wr.astype(BF16), h2.astype(BF16)
    wr_lo, h_lo = (wr - wr_hi.astype(F32)).astype(BF16), (h2 - h_hi.astype(F32)).astype(BF16)
    nt = lambda a, b: lax.dot_general(a, b, (((1,), (1,)), ((), ())), preferred_element_type=F32)
    both = nt(jnp.concatenate([wr_hi, wr_lo], axis=0), h_hi)
    logits = both[:N_EXPERTS] + (both[N_EXPERTS:] + nt(wr_hi, h_lo))
    cls, w_lo, w_hi = _route(logits, br_ref[...])
    cls_ref[...] = cls
    row = lax.broadcasted_iota(I32, (LANES, tm), 0)
    h2_ref[:, d:] = jnp.where(row == 0, w_lo, jnp.where(row == 1, w_hi, 0.0)).T


def _outproj_route(x2, ya, yb, yc, wa, wb, wc, ga1, nw, sc2, sh2, wrt, br, seq_len):
    n, d = x2.shape
    tm = min(ROW_TILE, n)
    row = lambda i: (i, 0)
    seq = _seq_block(tm, seq_len, d)
    const = lambda i: (0, 0)
    return pl.pallas_call(
        _outproj_body,
        grid=(n // tm,),
        in_specs=[pl.BlockSpec((tm, d), row),
                  pl.BlockSpec((tm, SSD_D), row),
                  pl.BlockSpec((tm, GMLP_D), row),
                  pl.BlockSpec((tm, GDN_V_D), row),
                  pl.BlockSpec((SSD_D, d), const),
                  pl.BlockSpec((GMLP_D, d), const),
                  pl.BlockSpec((GDN_V_D, d), const),
                  seq,
                  pl.BlockSpec((1, d), const),
                  seq,
                  seq,
                  pl.BlockSpec((N_EXPERTS, d), const),
                  pl.BlockSpec((N_EXPERTS, 1), const)],
        out_specs=[pl.BlockSpec((tm, d), row),
                   pl.BlockSpec((tm, d + LANES), row),
                   pl.BlockSpec((1, tm), lambda i: (0, i))],
        out_shape=[jax.ShapeDtypeStruct((n, d), F32),
                   jax.ShapeDtypeStruct((n, d + LANES), F32),
                   jax.ShapeDtypeStruct((1, n), I32)],
        compiler_params=_cparams("parallel"),
        name="outproj_route",
    )(x2, ya, yb, yc, wa, wb, wc, ga1, nw, sc2, sh2, wrt, br)


def _positions_body(cls_ref, dest_ref, blk_ref, nblk_ref, zrow_ref, cnt_ref, run_ref, *, nb_total):
    phase = pl.program_id(0)
    i = pl.program_id(1)
    t = cls_ref.shape[1]
    m0 = lax.broadcasted_iota(I32, (CLASS_ROWS, t), 0) == cls_ref[...]
    hits = jnp.where(m0, 1.0, 0.0)
    tile_cnt = jnp.broadcast_to(jnp.sum(hits, axis=1, keepdims=True), (CLASS_ROWS, LANES))

    @pl.when(jnp.logical_and(phase == 0, i == 0))
    def _():
        cnt_ref[...] = jnp.zeros_like(cnt_ref)

    @pl.when(phase == 0)
    def _():
        cnt_ref[...] += tile_cnt

    @pl.when(jnp.logical_and(phase == 1, i == 0))
    def _():
        padded = jnp.floor((cnt_ref[...] + (MOE_BLOCK - 1)) * (1.0 / MOE_BLOCK)) * MOE_BLOCK
        pad_end = _cumsum_rows(padded)
        run_ref[...] = pad_end - padded
        nb = blk_ref.shape[1]
        blk_start = (lax.broadcasted_iota(I32, (CLASS_ROWS, nb), 1) * MOE_BLOCK).astype(F32)
        past = jnp.where(blk_start >= pad_end[:, 0:1], 1.0, 0.0)
        cls = jnp.minimum(jnp.sum(past, axis=0, keepdims=True), N_CLASSES - 1.0)
        grp = jnp.floor(cls * (1.0 / PAIRS_PER_GROUP))
        pair = cls - grp * PAIRS_PER_GROUP
        lo = jnp.where(pair >= 3.0, 1.0, 0.0) + jnp.where(pair >= 5.0, 1.0, 0.0)
        hi = jnp.where(pair == 0.0, 1.0, jnp.where(jnp.logical_or(pair == 1.0, pair == 3.0), 2.0, 3.0))
        blk_ref[0:1, :] = (grp * EXPERTS_PER_GROUP + lo).astype(I32)
        blk_ref[1:2, :] = (grp * EXPERTS_PER_GROUP + hi).astype(I32)
        nblk = pad_end[CLASS_ROWS - 1:CLASS_ROWS, :] * (1.0 / MOE_BLOCK)
        nblk_ref[...] = nblk.astype(I32)
        row = lax.broadcasted_iota(I32, (CLASS_ROWS, LANES), 0)
        lane = lax.broadcasted_iota(I32, (CLASS_ROWS, LANES), 1)
        last_blk = jnp.where(padded > 0.0, pad_end - MOE_BLOCK, -1.0)
        per_class = jnp.sum(jnp.where(row == lane, last_blk, 0.0), axis=0, keepdims=True)
        lane1 = lane[0:1, :]
        tail_blk = nblk + (lane1 - CLASS_ROWS).astype(F32)
        tail = jnp.where(tail_blk < nb_total, tail_blk * MOE_BLOCK, -1.0)
        zrow = jnp.where(lane1 < CLASS_ROWS, per_class, jnp.where(lane1 < 2 * CLASS_ROWS, tail, -1.0))
        zrow_ref[...] = zrow.astype(I32)

    @pl.when(phase == 1)
    def _():
        excl = _cumsum_lanes(hits) - hits
        pos = run_ref[:, 0:1] + excl
        dest_ref[...] = jnp.sum(jnp.where(m0, pos, 0.0), axis=0, keepdims=True).astype(I32)
        run_ref[...] += tile_cnt


def _positions(cls, n_blocks):
    n = cls.shape[1]
    t = min(POS_TILE, n)
    nb_pad = -(-n_blocks // LANES) * LANES
    return pl.pallas_call(
        functools.partial(_positions_body, nb_total=n_blocks),
        grid=(2, n // t),
        in_specs=[pl.BlockSpec((1, t), lambda p, i: (0, i))],
        out_specs=[pl.BlockSpec((1, t), lambda p, i: (0, i * p)),
                   pl.BlockSpec((2, nb_pad), lambda p, i: (0, 0)),
                   pl.BlockSpec((1, LANES), lambda p, i: (0, 0)),
                   pl.BlockSpec((1, LANES), lambda p, i: (0, 0))],
        out_shape=[jax.ShapeDtypeStruct((1, n), I32),
                   jax.ShapeDtypeStruct((2, nb_pad), I32),
                   jax.ShapeDtypeStruct((1, LANES), I32),
                   jax.ShapeDtypeStruct((1, LANES), I32)],
        scratch_shapes=[pltpu.VMEM((CLASS_ROWS, LANES), F32), pltpu.VMEM((CLASS_ROWS, LANES), F32)],
        compiler_params=_cparams("arbitrary", "arbitrary"),
        name="moe_positions",
    )(cls)


def _row_copy(src_ref, src_row, dst_ref, dst_row, sem):
    return pltpu.make_async_copy(src_ref.at[pl.ds(src_row, 1), :], dst_ref.at[pl.ds(dst_row, 1), :], sem)


def _dispatch_body(dest_ref, zrow_ref, h_ref, xs_ref, zero_ref, sem, zsem):
    t = dest_ref.shape[1]

    @pl.when(pl.program_id(0) == 0)
    def _():
        zero_ref[...] = jnp.zeros_like(zero_ref)

        def zero_copy(k):
            start = pl.multiple_of(zrow_ref[0, k], MOE_BLOCK)
            return pltpu.make_async_copy(zero_ref, xs_ref.at[pl.ds(start, MOE_BLOCK), :], zsem)

        for k in range(2 * CLASS_ROWS):
            @pl.when(zrow_ref[0, k] >= 0)
            def _():
                zero_copy(k).start()

        for k in range(2 * CLASS_ROWS):
            @pl.when(zrow_ref[0, k] >= 0)
            def _():
                zero_copy(k).wait()

    def issue(jo, carry):
        for k in range(DMA_UNROLL):
            j = jo * DMA_UNROLL + k
            _row_copy(h_ref, j, xs_ref, dest_ref[0, j], sem).start()
        return carry

    lax.fori_loop(0, t // DMA_UNROLL, issue, 0)
    pltpu.make_async_copy(h_ref, xs_ref.at[pl.ds(0, t), :], sem).wait()


def _dispatch(dest, zrow, h2, n_rows):
    n, d = h2.shape
    t = min(DMA_TILE, n)
    return pl.pallas_call(
        _dispatch_body,
        grid=(n // t,),
        in_specs=[pl.BlockSpec((1, t), lambda i: (0, i), memory_space=pltpu.SMEM),
                  pl.BlockSpec((1, LANES), lambda i: (0, 0), memory_space=pltpu.SMEM),
                  pl.BlockSpec((t, d), lambda i: (i, 0))],
        out_specs=pl.BlockSpec(memory_space=pl.ANY),
        out_shape=jax.ShapeDtypeStruct((n_rows, d), F32),
        scratch_shapes=[pltpu.VMEM((MOE_BLOCK, d), F32), pltpu.SemaphoreType.DMA(()), pltpu.SemaphoreType.DMA(())],
        compiler_params=_cparams("arbitrary"),
        name="moe_dispatch",
    )(dest, zrow, h2)


def _experts_body(blk_lo_ref, blk_hi_ref, nblk_ref, xs_ref, wg_lo_ref, wu_lo_ref, wd_lo_ref,
                  wg_hi_ref, wu_hi_ref, wd_hi_ref, y_ref):
    used = pl.program_id(0) < nblk_ref[0]
    d = y_ref.shape[1]

    @pl.when(used)
    def _():
        x = xs_ref[:, :d].astype(BF16)
        wts = xs_ref[:, d:]

        def ffn(wg_ref, wu_ref, wd_ref):
            hid = _silu(jnp.dot(x, wg_ref[0], preferred_element_type=F32)) * jnp.dot(x, wu_ref[0], preferred_element_type=F32)
            return jnp.dot(hid.astype(BF16), wd_ref[0], preferred_element_type=F32)

        y_ref[...] = ffn(wg_lo_ref, wu_lo_ref, wd_lo_ref) * wts[:, 0:1] + ffn(wg_hi_ref, wu_hi_ref, wd_hi_ref) * wts[:, 1:2]

    @pl.when(jnp.logical_not(used))
    def _():
        y_ref[...] = jnp.zeros_like(y_ref)


def _experts(blk, nblk, xs, wg, wu, wd):
    n_rows = xs.shape[0]
    d = wg.shape[1]
    nb = n_rows // MOE_BLOCK

    def live(b, nbl):
        return jnp.minimum(b, nbl[0] - 1)

    rows = lambda b, lo, hi, nbl: (live(b, nbl), 0)
    w_lo = lambda b, lo, hi, nbl: (lo[live(b, nbl)], 0, 0)
    w_hi = lambda b, lo, hi, nbl: (hi[live(b, nbl)], 0, 0)
    return pl.pallas_call(
        _experts_body,
        grid_spec=pltpu.PrefetchScalarGridSpec(
            num_scalar_prefetch=3,
            grid=(nb,),
            in_specs=[pl.BlockSpec((MOE_BLOCK, xs.shape[1]), rows),
                      pl.BlockSpec((1, d, D_EXPERT), w_lo),
                      pl.BlockSpec((1, d, D_EXPERT), w_lo),
                      pl.BlockSpec((1, D_EXPERT, d), w_lo),
                      pl.BlockSpec((1, d, D_EXPERT), w_hi),
                      pl.BlockSpec((1, d, D_EXPERT), w_hi),
                      pl.BlockSpec((1, D_EXPERT, d), w_hi)],
            out_specs=pl.BlockSpec((MOE_BLOCK, d), lambda b, lo, hi, nbl: (b, 0))),
        out_shape=jax.ShapeDtypeStruct((n_rows, d), F32),
        compiler_params=_cparams("arbitrary"),
        name="moe_experts",
    )(blk[0], blk[1], nblk, xs, wg, wu, wd, wg, wu, wd)


def _combine_body(dest_ref, x_ref, ga_ref, fw_ref, yb_ref, o_ref, r_ref, sem, *, final_norm):
    t = dest_ref.shape[1]

    def issue(jo, carry):
        for k in range(DMA_UNROLL):
            j = jo * DMA_UNROLL + k
            _row_copy(yb_ref, dest_ref[0, j], r_ref, j, sem).start()
        return carry

    lax.fori_loop(0, t // DMA_UNROLL, issue, 0)
    pltpu.make_async_copy(yb_ref.at[pl.ds(0, t), :], r_ref, sem).wait()
    x = x_ref[...] + _seq_rows(ga_ref, t) * r_ref[...]
    if final_norm:
        x = x * lax.rsqrt(jnp.mean(x * x, axis=-1, keepdims=True) + EPS) * fw_ref[...]
    o_ref[...] = x


def _combine(dest, x2, ga2, fw, yb, seq_len, final_norm):
    n, d = x2.shape
    t = min(DMA_TILE, n)
    row = lambda i: (i, 0)
    return pl.pallas_call(
        functools.partial(_combine_body, final_norm=final_norm),
        grid=(n // t,),
        in_specs=[pl.BlockSpec((1, t), lambda i: (0, i), memory_space=pltpu.SMEM),
                  pl.BlockSpec((t, d), row),
                  _seq_block(t, seq_len, d),
                  pl.BlockSpec((1, d), lambda i: (0, 0)),
                  pl.BlockSpec(memory_space=pl.ANY)],
        out_specs=pl.BlockSpec((t, d), row),
        out_shape=jax.ShapeDtypeStruct((n, d), F32),
        scratch_shapes=[pltpu.VMEM((t, d), F32), pltpu.SemaphoreType.DMA(())],
        compiler_params=_cparams("arbitrary"),
        name="moe_combine",
    )(dest, x2, ga2, fw, yb)


def _lane_vec(vals, lane):
    return jnp.zeros((1, LANES), F32).at[0, lane:lane + vals.shape[0]].set(vals.astype(F32))


def _w_in_layout_body(w_ref, small_ref, o_ref):
    for name, i in WIDE_SEGMENTS:
        dst, width = COLS[name]
        o_ref[0, :, dst:dst + width] = w_ref[0, :, IN_OFFSETS[i]:IN_OFFSETS[i] + width].astype(BF16)
    dst, width = COLS["small"]
    o_ref[0, :, dst:dst + width] = small_ref[0]


def _w_in_layout(w_in):
    depth, d, in_d = w_in.shape
    small = jnp.zeros((depth, d, LANES), BF16)
    for i, lane in GATE_COLUMNS:
        small = small.at[:, :, lane:lane + IN_SIZES[i]].set(w_in[:, :, IN_OFFSETS[i]:IN_OFFSETS[i] + IN_SIZES[i]].astype(BF16))
    tr = 256
    return pl.pallas_call(
        _w_in_layout_body,
        grid=(depth, d // tr),
        in_specs=[pl.BlockSpec((1, tr, in_d), lambda l, i: (l, i, 0)),
                  pl.BlockSpec((1, tr, LANES), lambda l, i: (l, i, 0))],
        out_specs=pl.BlockSpec((1, tr, IN_R), lambda l, i: (l, i, 0)),
        out_shape=jax.ShapeDtypeStruct((depth, d, IN_R), BF16),
        compiler_params=_cparams("parallel", "parallel"),
        name="w_in_layout",
    )(w_in, small)


def _layer_params(p, l, w_r):
    w_out = p["w_out"][l].astype(BF16)
    return dict(
        w_r=w_r[l],
        wa=w_out[:SSD_D], wb=w_out[SSD_D:SSD_D + GMLP_D], wc=w_out[SSD_D + GMLP_D:],
        norm_mix=p["norm_mix"][l][None, :], norm_ffn=p["norm_ffn"][l][None, :],
        ssd_cw=p["ssd_conv_w"][l], ssd_cb=p["ssd_conv_b"][l][None, :],
        ssd_dtb=_lane_vec(p["ssd_dt_bias"][l], SSD_DT_LANE), ssd_alog=_lane_vec(p["ssd_a_log"][l], SSD_DT_LANE),
        ssd_dvec=jnp.repeat(p["ssd_d"][l].astype(F32), SSD_HEAD_DIM)[None, :], ssd_nw=p["ssd_norm_w"][l][None, :],
        lng=p["gmlp_ln_g"][l][None, :], lnb=p["gmlp_ln_b"][l][None, :], ws=p["gmlp_ws"][l], bs=p["gmlp_bs"][l],
        gdn_cw=p["gdn_conv_w"][l],
        gdn_dtb=_lane_vec(p["gdn_dt_bias"][l], GDN_A_LANE), gdn_alog=_lane_vec(p["gdn_a_log"][l], GDN_A_LANE),
        gdn_nw=p["gdn_norm_w"][l][None, :],
        wg=p["w_gate"][l].astype(BF16), wu=p["w_up"][l].astype(BF16), wd=p["w_down"][l].astype(BF16),
    )


def _trunk(x, mod, ssd_h0, ssd_buf0, gdn_s0, gdn_buf0, p, layers):
    bsz, seq_len, d = x.shape
    n = bsz * seq_len
    x2 = x.reshape(n, d)
    n_rows = -(-n // MOE_BLOCK) * MOE_BLOCK + N_CLASSES * MOE_BLOCK
    qg = min(GMLP_CHUNK, seq_len)
    wrt = p["w_router"].T.astype(F32)
    br = p["b_router"].astype(F32)[:, None]
    fw = p["norm_final"][None, :]
    outs = ([], [], [], [], [])
    depth = len(layers)
    for l, lp in enumerate(layers):
        sh1, sc1, ga1, sh2, sc2, ga2 = (m[:, None, :] for m in jnp.split(mod[l], 6, axis=-1))
        proj = _inproj(x2, sh1, sc1, lp["norm_mix"], lp["w_r"], seq_len)
        ya, ssd_ht, ssd_buf = _ssd(proj, jnp.swapaxes(ssd_h0[l], -1, -2), ssd_buf0[l], lp["ssd_cw"], lp["ssd_cb"],
                                   lp["ssd_dtb"], lp["ssd_alog"], lp["ssd_dvec"], lp["ssd_nw"], bsz, seq_len)
        yb, vn = _gmlp(proj, lp["lng"], lp["lnb"], lp["ws"][:, :qg, :qg], lp["bs"][:, :qg].T, bsz, seq_len)
        yc, gdn_s, gdn_buf = _gdn(proj, gdn_s0[l], gdn_buf0[l], lp["gdn_cw"], lp["gdn_dtb"], lp["gdn_alog"],
                                  lp["gdn_nw"], bsz, seq_len)
        x2, h2w, cls = _outproj_route(x2, ya, yb, yc, lp["wa"], lp["wb"], lp["wc"], ga1, lp["norm_ffn"],
                                      sc2, sh2, wrt, br, seq_len)
        dest, blk, nblk, zrow = _positions(cls, n_rows // MOE_BLOCK)
        xs = _dispatch(dest, zrow, h2w, n_rows)
        ye = _experts(blk, nblk[0, :1], xs, lp["wg"], lp["wu"], lp["wd"])
        x2 = _combine(dest, x2, ga2, fw, ye, seq_len, final_norm=(l == depth - 1))
        for lst, s in zip(outs, (jnp.swapaxes(ssd_ht, -1, -2), ssd_buf, gdn_s, gdn_buf, vn.reshape(bsz, seq_len, GMLP_D))):
            lst.append(s)
    return (x2.reshape(bsz, seq_len, d),) + tuple(jnp.stack(o) for o in outs)


def kernel(x_prompt, x_sample, state_ssd, state_ssd_conv, state_gdn, state_gdn_conv, c_prompt, c_sample,
           w_ada, b_ada, norm_mix, norm_ffn, norm_final, w_in, w_out,
           ssd_conv_w, ssd_conv_b, ssd_dt_bias, ssd_a_log, ssd_d, ssd_norm_w,
           gmlp_ln_g, gmlp_ln_b, gmlp_ws, gmlp_bs,
           gdn_conv_w, gdn_a_log, gdn_dt_bias, gdn_norm_w,
           w_router, b_router, w_gate, w_up, w_down):
    p = dict(norm_mix=norm_mix, norm_ffn=norm_ffn, norm_final=norm_final, w_in=w_in, w_out=w_out,
             ssd_conv_w=ssd_conv_w, ssd_conv_b=ssd_conv_b, ssd_dt_bias=ssd_dt_bias, ssd_a_log=ssd_a_log,
             ssd_d=ssd_d, ssd_norm_w=ssd_norm_w, gmlp_ln_g=gmlp_ln_g, gmlp_ln_b=gmlp_ln_b, gmlp_ws=gmlp_ws,
             gmlp_bs=gmlp_bs, gdn_conv_w=gdn_conv_w, gdn_a_log=gdn_a_log, gdn_dt_bias=gdn_dt_bias,
             gdn_norm_w=gdn_norm_w, w_router=w_router, b_router=b_router, w_gate=w_gate, w_up=w_up, w_down=w_down)
    depth = w_in.shape[0]
    w_r = _w_in_layout(w_in)
    layers = [_layer_params(p, l, w_r) for l in range(depth)]
    bp = x_prompt.shape[0]
    mod = _ada_mod(jnp.concatenate([c_prompt, c_sample], axis=0), w_ada.astype(BF16), b_ada)
    z_ssd = jnp.zeros((depth, bp) + state_ssd.shape[2:], state_ssd.dtype)
    z_ssd_conv = jnp.zeros((depth, bp) + state_ssd_conv.shape[2:], x_prompt.dtype)
    z_gdn = jnp.zeros((depth, bp) + state_gdn.shape[2:], state_gdn.dtype)
    z_gdn_conv = jnp.zeros((depth, bp) + state_gdn_conv.shape[2:], x_prompt.dtype)
    y_p, p_ssd, p_ssd_conv, p_gdn, p_gdn_conv, _ = _trunk(
        x_prompt, mod[:, :bp], z_ssd, z_ssd_conv, z_gdn, z_gdn_conv, p, layers)
    y_s, s_ssd, s_ssd_conv, s_gdn, s_gdn_conv, s_gmlp_v = _trunk(
        x_sample, mod[:, bp:], state_ssd, state_ssd_conv, state_gdn, state_gdn_conv, p, layers)
    return (y_p, y_s, p_ssd, p_ssd_conv, p_gdn, p_gdn_conv, s_ssd, s_ssd_conv, s_gdn, s_gdn_conv, s_gmlp_v)
```
